```python
import math
import jax, jax.numpy as jnp
from jax import lax
import numpy as np

D_MODEL = 2048
BATCH = 4
SEQ = 4096
DEPTH = 2

N_EVEN = (DEPTH + 1) // 2
N_ODD = DEPTH // 2

SSD_WIDTH = D_MODEL
SSD_HEAD_DIM = 64
SSD_HEADS = SSD_WIDTH // SSD_HEAD_DIM
SSD_GROUPS = 4
SSD_STATE = 128
SSD_CONV = 4
SSD_CHUNK = 256
SSD_CONV_DIM = SSD_WIDTH + 2 * SSD_GROUPS * SSD_STATE

ATT_HEADS = 16
ATT_HEAD_DIM = 128
ATT_WIDTH = ATT_HEADS * ATT_HEAD_DIM
MOBA_BLOCK = 256
MOBA_TOPK = 3
MOBA_Q_CHUNK = 64
ROPE_THETA = 500000.0
ROPE_DIM = ATT_HEAD_DIM // 4

_Z_END = SSD_WIDTH
_XBC_END = _Z_END + SSD_CONV_DIM
_DT_END = _XBC_END + SSD_HEADS
_Q_END = _DT_END + ATT_WIDTH
_K_END = _Q_END + ATT_WIDTH
_V_END = _K_END + ATT_WIDTH
IN0_WIDTH = _V_END + ATT_WIDTH
SPLITS0 = (_Z_END, _XBC_END, _DT_END, _Q_END, _K_END, _V_END)
MIX0_WIDTH = SSD_WIDTH + ATT_WIDTH

S5_WIDTH = D_MODEL
S5_GROUP = 16
S5_GROUPS = S5_WIDTH // S5_GROUP
S5_STATE = 64
S5_SCAN_CHUNK = 128
S5_C_STD = 0.5

DEEPNORM_ALPHA = (2 * DEPTH) ** 0.25
DEEPNORM_BETA = (8 * DEPTH) ** -0.25
LN_EPS = 1e-5
RMS_EPS = 1e-5
NEG_INF = -1e30

kernel_name = 'hybrid_ssd_moba_s5_deepnorm'


def layer_norm(x, g, b):
    xf = x.astype(jnp.float32)
    mu = jnp.mean(xf, axis=-1, keepdims=True)
    var = jnp.mean(jnp.square(xf - mu), axis=-1, keepdims=True)
    y = (xf - mu) * lax.rsqrt(var + LN_EPS) * g.astype(jnp.float32) + b.astype(jnp.float32)
    return y.astype(x.dtype)


def rms_norm(x, g):
    xf = x.astype(jnp.float32)
    return xf * lax.rsqrt(jnp.mean(jnp.square(xf), axis=-1, keepdims=True) + RMS_EPS) * g.astype(jnp.float32)


def causal_depthwise_conv(x, w, b):
    k = w.shape[0]
    y = lax.conv_general_dilated(x, w[:, None, :].astype(x.dtype), window_strides=(1,),
                                 padding=[(k - 1, 0)], dimension_numbers=('NWC', 'WIO', 'NWC'),
                                 feature_group_count=x.shape[-1])
    return y + b.astype(x.dtype)


def segsum(a):
    t = a.shape[-1]
    cs = jnp.cumsum(a, axis=-1)
    diff = cs[..., :, None] - cs[..., None, :]
    return jnp.where(jnp.tril(jnp.ones((t, t), dtype=bool)), diff, -jnp.inf)


def ssd_chunked_scan(x, a, b, c):
    bsz, s, h, p = x.shape
    g, n = b.shape[-2:]
    r = h // g
    pad = (-s) % SSD_CHUNK
    x = jnp.pad(x, ((0, 0), (0, pad), (0, 0), (0, 0)))
    a = jnp.pad(a, ((0, 0), (0, pad), (0, 0)))
    b = jnp.pad(b, ((0, 0), (0, pad), (0, 0), (0, 0)))
    c = jnp.pad(c, ((0, 0), (0, pad), (0, 0), (0, 0)))
    nc, l = (s + pad) // SSD_CHUNK, SSD_CHUNK
    x = x.reshape(bsz, nc, l, g, r, p)
    a = a.reshape(bsz, nc, l, g, r).transpose(0, 3, 4, 1, 2)
    b = b.reshape(bsz, nc, l, g, n)
    c = c.reshape(bsz, nc, l, g, n)
    a_cum = jnp.cumsum(a, axis=-1)
    decay = jnp.exp(segsum(a))
    cb = jnp.einsum('bclgn,bcsgn->bcgls', c, b)
    y_diag = jnp.einsum('bcgls,bgrcls,bcsgrp->bclgrp', cb, decay, x)
    decay_to_end = jnp.exp(a_cum[..., -1:] - a_cum)
    states = jnp.einsum('bclgn,bgrcl,bclgrp->bcgrpn', b, decay_to_end, x)
    states = jnp.concatenate([jnp.zeros_like(states[:, :1]), states], axis=1)
    chunk_decay = jnp.exp(segsum(jnp.pad(a_cum[..., -1], ((0, 0), (0, 0), (0, 0), (1, 0)))))
    states = jnp.einsum('bgrzc,bcgrpn->bzgrpn', chunk_decay, states)[:, :-1]
    y_off = jnp.einsum('bclgn,bcgrpn,bgrcl->bclgrp', c, states, jnp.exp(a_cum))
    y = (y_diag + y_off).reshape(bsz, nc * l, h, p)
    return y[:, :s]


def apply_partial_rotary(x, pos):
    half = ROPE_DIM // 2
    inv_freq = ROPE_THETA ** (-(jnp.arange(half, dtype=jnp.float32) * 2.0 / ROPE_DIM))
    ang = pos.astype(jnp.float32)[:, None] * inv_freq[None, :]
    cos = jnp.cos(ang)[None, :, None, :]
    sin = jnp.sin(ang)[None, :, None, :]
    xr = x[..., :ROPE_DIM].astype(jnp.float32)
    x1, x2 = xr[..., :half], xr[..., half:]
    rot = jnp.concatenate([x1 * cos - x2 * sin, x1 * sin + x2 * cos], axis=-1)
    return jnp.concatenate([rot.astype(x.dtype), x[..., ROPE_DIM:]], axis=-1)


def moba_attention(q, k, v):
    bsz, s, h, dh = q.shape
    s_pad = -(-s // MOBA_BLOCK) * MOBA_BLOCK
    padw = ((0, 0), (0, s_pad - s), (0, 0), (0, 0))
    q, k, v = jnp.pad(q, padw), jnp.pad(k, padw), jnp.pad(v, padw)
    nb = s_pad // MOBA_BLOCK
    n_sel = min(MOBA_TOPK, nb - 1)
    scale = dh ** -0.5
    kb = k.reshape(bsz, nb, MOBA_BLOCK, h, dh).transpose(0, 3, 1, 2, 4)
    vb = v.reshape(bsz, nb, MOBA_BLOCK, h, dh).transpose(0, 3, 1, 2, 4)
    k_mean = jnp.mean(kb.astype(jnp.float32), axis=3)
    b_ix = jnp.arange(bsz)[:, None, None]
    h_ix = jnp.arange(h)[None, None, :]
    blk_ids = jnp.arange(nb)

    def query_chunk(ci):
        start = ci * MOBA_Q_CHUNK
        blk = start // MOBA_BLOCK
        qc = lax.dynamic_slice_in_dim(q, start, MOBA_Q_CHUNK, axis=1).astype(jnp.float32)
        qpos = start + jnp.arange(MOBA_Q_CHUNK)
        blk_start = blk * MOBA_BLOCK
        k_own = lax.dynamic_slice_in_dim(k, blk_start, MOBA_BLOCK, axis=1).astype(jnp.float32)
        v_own = lax.dynamic_slice_in_dim(v, blk_start, MOBA_BLOCK, axis=1).astype(jnp.float32)
        kpos = blk_start + jnp.arange(MOBA_BLOCK)
        s_own = jnp.einsum('bqhd,bkhd->bqhk', qc, k_own) * scale
        causal = (kpos[None, :] <= qpos[:, None])[None, :, None, :]
        s_own = jnp.where(causal, s_own, NEG_INF)
        if n_sel == 0:
            p = jax.nn.softmax(s_own, axis=-1)
            return jnp.einsum('bqhk,bkhd->bqhd', p, v_own).astype(q.dtype)
        gate = jnp.einsum('bqhd,bhnd->bqhn', qc, k_mean)
        gate = jnp.where((blk_ids < blk)[None, None, None, :], gate, NEG_INF)
        _, top_i = lax.top_k(gate, n_sel)
        valid = jnp.arange(n_sel) < blk
        scores = []
        for r in range(n_sel):
            k_r = kb[b_ix, h_ix, top_i[..., r]].astype(jnp.float32)
            s_r = jnp.einsum('bqhd,bqhkd->bqhk', qc, k_r) * scale
            scores.append(jnp.where(valid[r], s_r, NEG_INF))
        scores.append(s_own)
        p = jax.nn.softmax(jnp.concatenate(scores, axis=-1), axis=-1)
        out = jnp.einsum('bqhk,bkhd->bqhd', p[..., n_sel * MOBA_BLOCK:], v_own)
        for r in range(n_sel):
            v_r = vb[b_ix, h_ix, top_i[..., r]].astype(jnp.float32)
            out = out + jnp.einsum('bqhk,bqhkd->bqhd', p[..., r * MOBA_BLOCK:(r + 1) * MOBA_BLOCK], v_r)
        return out.astype(q.dtype)

    out = lax.map(query_chunk, jnp.arange(s_pad // MOBA_Q_CHUNK))
    out = out.transpose(1, 0, 2, 3, 4).reshape(bsz, s_pad, h, dh)
    return out[:, :s]


def ssd_moba_mixer(x, w_in, conv_w, conv_b, dt_bias, a_log, d_skip, norm_g, w_out):
    bsz, s, _ = x.shape
    f32 = jnp.float32
    z, xbc, dt, q, k, v, gate = jnp.split(x @ w_in, SPLITS0, axis=-1)
    xbc = jax.nn.silu(causal_depthwise_conv(xbc, conv_w, conv_b))
    xs, bm, cm = jnp.split(xbc, (SSD_WIDTH, SSD_WIDTH + SSD_GROUPS * SSD_STATE), axis=-1)
    dt = jax.nn.softplus(dt.astype(f32) + dt_bias.astype(f32))
    a = -jnp.exp(a_log.astype(f32))
    xh = xs.astype(f32).reshape(bsz, s, SSD_HEADS, SSD_HEAD_DIM)
    y = ssd_chunked_scan(xh * dt[..., None], dt * a,
                         bm.astype(f32).reshape(bsz, s, SSD_GROUPS, SSD_STATE),
                         cm.astype(f32).reshape(bsz, s, SSD_GROUPS, SSD_STATE))
    y = (y + d_skip.astype(f32)[:, None] * xh).reshape(bsz, s, SSD_WIDTH)
    y_a = rms_norm(y * jax.nn.silu(z.astype(f32)), norm_g).astype(x.dtype)
    pos = jnp.arange(s)
    q = apply_partial_rotary(q.reshape(bsz, s, ATT_HEADS, ATT_HEAD_DIM), pos)
    k = apply_partial_rotary(k.reshape(bsz, s, ATT_HEADS, ATT_HEAD_DIM), pos)
    v = v.reshape(bsz, s, ATT_HEADS, ATT_HEAD_DIM)
    y_b = moba_attention(q, k, v).reshape(bsz, s, ATT_WIDTH) * jax.nn.silu(gate)
    return jnp.concatenate([y_a, y_b], axis=-1) @ w_out


def _linear_recurrence(e1, e2):
    a1, b1 = e1
    a2, b2 = e2
    return a1 * a2, a2 * b1 + b2


def s5_ssm(u, lam_re, lam_im, log_dt, b_re, b_im, c_re, c_im, d_skip):
    bsz, s, w = u.shape
    f32 = jnp.float32
    lam = lax.complex(lam_re.astype(f32), lam_im.astype(f32))
    dt = jnp.exp(log_dt.astype(f32))[:, None]
    lam_bar = jnp.exp(lam * dt)
    b_bar = ((lam_bar - 1.0) / lam)[..., None] * lax.complex(b_re.astype(f32), b_im.astype(f32))
    c_mat = lax.complex(c_re.astype(f32), c_im.astype(f32))
    nch = s // S5_SCAN_CHUNK
    ug = u.astype(f32).reshape(bsz, nch, S5_SCAN_CHUNK, S5_GROUPS, S5_GROUP).transpose(1, 0, 2, 3, 4)

    def step(h, u_blk):
        bu = jnp.einsum('gnm,blgm->blgn', b_bar, u_blk.astype(jnp.complex64))
        bu = bu.at[:, 0].add(lam_bar * h)
        a = jnp.broadcast_to(lam_bar, bu.shape)
        _, hs = lax.associative_scan(_linear_recurrence, (a, bu), axis=1)
        y = jnp.einsum('gmn,blgn->blgm', c_mat, hs).real
        return hs[:, -1], y

    h0 = jnp.zeros((bsz, S5_GROUPS, S5_STATE), jnp.complex64)
    _, ys = lax.scan(step, h0, ug)
    y = ys.transpose(1, 0, 2, 3, 4).reshape(bsz, s, w)
    return (y + d_skip.astype(f32) * u.astype(f32)).astype(u.dtype)


def s5_mixer(x, w_in, lam_re, lam_im, log_dt, b_re, b_im, c_re, c_im, d_skip, w_glu, w_out):
    u, gate = jnp.split(x @ w_in, 2, axis=-1)
    y = jax.nn.gelu(s5_ssm(u, lam_re, lam_im, log_dt, b_re, b_im, c_re, c_im, d_skip))
    ga, gb = jnp.split(y @ w_glu, 2, axis=-1)
    y = ga * jax.nn.sigmoid(gb)
    return (y * jax.nn.silu(gate)) @ w_out


def setup_inputs(seed: int = 0) -> dict:
    key = jax.random.key(seed)
    ks = jax.random.split(key, 24)
    f32 = jnp.float32

    def nrm(k, shape, std):
        return jax.random.normal(k, shape, f32) * std

    x = nrm(ks[0], (BATCH, SEQ, D_MODEL), 1.0)
    in0_w = nrm(ks[1], (N_EVEN, D_MODEL, IN0_WIDTH), D_MODEL ** -0.5)
    conv_w = nrm(ks[2], (N_EVEN, SSD_CONV, SSD_CONV_DIM), SSD_CONV ** -0.5)
    conv_b = nrm(ks[3], (N_EVEN, SSD_CONV_DIM), 0.01)
    dt0 = jnp.exp(jax.random.uniform(ks[4], (N_EVEN, SSD_HEADS), f32, math.log(1e-3), math.log(1e-1)))
    dt_bias = dt0 + jnp.log(-jnp.expm1(-dt0))
    a_log = jnp.log(jax.random.uniform(ks[5], (N_EVEN, SSD_HEADS), f32, 1.0, 16.0))
    ssd_d = 1.0 + nrm(ks[6], (N_EVEN, SSD_HEADS), 0.1)
    ssd_norm_g = 1.0 + nrm(ks[7], (N_EVEN, SSD_WIDTH), 0.02)
    out0_w = nrm(ks[8], (N_EVEN, MIX0_WIDTH, D_MODEL), DEEPNORM_BETA * MIX0_WIDTH ** -0.5)
    in1_w = nrm(ks[9], (N_ODD, D_MODEL, 2 * S5_WIDTH), D_MODEL ** -0.5)
    s5_lam_re = -0.5 + nrm(ks[10], (N_ODD, S5_GROUPS, S5_STATE), 0.01)
    s5_lam_im = math.pi * jnp.arange(S5_STATE, dtype=f32) + nrm(ks[11], (N_ODD, S5_GROUPS, S5_STATE), 0.01)
    s5_log_dt = jax.random.uniform(ks[12], (N_ODD, S5_GROUPS), f32, math.log(1e-3), math.log(1e-1))
    s5_b_re = nrm(ks[13], (N_ODD, S5_GROUPS, S5_STATE, S5_GROUP), (2 * S5_GROUP) ** -0.5)
    s5_b_im = nrm(ks[14], (N_ODD, S5_GROUPS, S5_STATE, S5_GROUP), (2 * S5_GROUP) ** -0.5)
    s5_c_re = nrm(ks[15], (N_ODD, S5_GROUPS, S5_GROUP, S5_STATE), S5_C_STD)
    s5_c_im = nrm(ks[16], (N_ODD, S5_GROUPS, S5_GROUP, S5_STATE), S5_C_STD)
    s5_d = nrm(ks[17], (N_ODD, S5_WIDTH), 1.0)
    glu_w = nrm(ks[18], (N_ODD, S5_WIDTH, 2 * S5_WIDTH), S5_WIDTH ** -0.5)
    out1_w = nrm(ks[19], (N_ODD, S5_WIDTH, D_MODEL), DEEPNORM_BETA * S5_WIDTH ** -0.5)
    ln_g = 1.0 + nrm(ks[20], (DEPTH, D_MODEL), 0.02)
    ln_b = nrm(ks[21], (DEPTH, D_MODEL), 0.02)
    return {'x': x, 'in0_w': in0_w, 'conv_w': conv_w, 'conv_b': conv_b, 'dt_bias': dt_bias,
            'a_log': a_log, 'ssd_d': ssd_d, 'ssd_norm_g': ssd_norm_g, 'out0_w': out0_w,
            'in1_w': in1_w, 's5_lam_re': s5_lam_re, 's5_lam_im': s5_lam_im, 's5_log_dt': s5_log_dt,
            's5_b_re': s5_b_re, 's5_b_im': s5_b_im, 's5_c_re': s5_c_re, 's5_c_im': s5_c_im,
            's5_d': s5_d, 'glu_w': glu_w, 'out1_w': out1_w, 'ln_g': ln_g, 'ln_b': ln_b}


def reference(x, in0_w, conv_w, conv_b, dt_bias, a_log, ssd_d, ssd_norm_g, out0_w,
              in1_w, s5_lam_re, s5_lam_im, s5_log_dt, s5_b_re, s5_b_im, s5_c_re, s5_c_im,
              s5_d, glu_w, out1_w, ln_g, ln_b):
    for layer in range(DEPTH):
        i = layer // 2
        if layer % 2 == 0:
            h = ssd_moba_mixer(x, in0_w[i], conv_w[i], conv_b[i], dt_bias[i], a_log[i],
                               ssd_d[i], ssd_norm_g[i], out0_w[i])
        else:
            h = s5_mixer(x, in1_w[i], s5_lam_re[i], s5_lam_im[i], s5_log_dt[i], s5_b_re[i],
                         s5_b_im[i], s5_c_re[i], s5_c_im[i], s5_d[i], glu_w[i], out1_w[i])
        x = layer_norm(DEEPNORM_ALPHA * x + h, ln_g[layer], ln_b[layer])
    return x
```

```python
import functools
import math

import jax
import jax.numpy as jnp
from jax import lax
from jax.experimental import pallas as pl
from jax.experimental.pallas import tpu as pltpu

F32 = jnp.float32
BF16 = jnp.bfloat16
HIGHEST = lax.Precision.HIGHEST

SSD_HEAD_DIM = 64
SSD_GROUPS = 4
SSD_STATE = 128
SSD_CONV = 4
SSD_CHUNK = 256
ATT_HEAD_DIM = 128
MOBA_BLOCK = 256
MOBA_TOPK = 3
ROPE_THETA = 500000.0
ROPE_DIM = ATT_HEAD_DIM // 4
S5_GROUP = 16
S5_STATE = 64
LN_EPS = 1e-5
RMS_EPS = 1e-5
NEG_INF = -1e30

LANES = 128
SUBLANES = 8
VMEM_LIMIT = 56 * 1024 * 1024

S5_L = 64


def _cparams(sem):
    return pltpu.CompilerParams(dimension_semantics=sem, vmem_limit_bytes=VMEM_LIMIT)


def _resident(shape, index_map):
    return pl.BlockSpec(shape, index_map, pipeline_mode=pl.Buffered(1))


def _silu(v):
    return v * (1.0 / (1.0 + jnp.exp(-v)))


def _nt_dot(a, b):
    return lax.dot_general(a, b, (((1,), (1,)), ((), ())), preferred_element_type=F32)


def _tn_dot(a, b):
    return lax.dot_general(a, b, (((0,), (0,)), ((), ())), preferred_element_type=F32)


def _dot_hi(a, b):
    return jnp.dot(a, b, preferred_element_type=F32, precision=HIGHEST)


def _inproj_kernel(x_ref, w_ref, wdt_ref, o_ref, dt_ref, xb_ref):
    @pl.when(pl.program_id(1) == 0)
    def _():
        xb = x_ref[...].astype(BF16)
        xb_ref[...] = xb
        dt_ref[...] = jnp.dot(xb, wdt_ref[...], preferred_element_type=F32)

    o_ref[...] = jnp.dot(xb_ref[...], w_ref[...], preferred_element_type=F32).astype(o_ref.dtype)


def _inproj(x2, w_main, w_dt, tm, tn):
    t, d = x2.shape
    n = w_main.shape[1]
    return pl.pallas_call(
        _inproj_kernel,
        grid=(t // tm, n // tn),
        in_specs=[
            pl.BlockSpec((tm, d), lambda i, j: (i, 0)),
            pl.BlockSpec((d, tn), lambda i, j: (0, j)),
            _resident((d, LANES), lambda i, j: (0, 0)),
        ],
        out_specs=[
            pl.BlockSpec((tm, tn), lambda i, j: (i, j)),
            pl.BlockSpec((tm, LANES), lambda i, j: (i, 0)),
        ],
        out_shape=[jax.ShapeDtypeStruct((t, n), BF16), jax.ShapeDtypeStruct((t, LANES), F32)],
        scratch_shapes=[pltpu.VMEM((tm, d), BF16)],
        compiler_params=_cparams(("parallel", "arbitrary")),
        name="inproj0",
    )(x2, w_main, w_dt)


def _ssd_kernel(z_ref, xs_ref, bm_ref, cm_ref, dt_ref, cw_ref, cb_ref, dtb_ref, aneg_ref, dsk_ref, ng_ref,
                exp_ref, o_ref, tail_ref, state_ref, act_ref, y_ref, *, n_groups, head_dim):
    c = pl.program_id(1)
    chunk, width = z_ref.shape
    d_state = bm_ref.shape[1] // n_groups
    heads_per_group = width // head_dim // n_groups
    gw = heads_per_group * head_dim

    @pl.when(c == 0)
    def _():
        tail_ref[...] = jnp.zeros_like(tail_ref)
        state_ref[...] = jnp.zeros_like(state_ref)

    strip = 512
    col0 = 0
    for src in (xs_ref, bm_ref, cm_ref):
        for s0 in range(0, src.shape[1], strip):
            cols = slice(col0 + s0, col0 + s0 + strip)
            xs_ = src[:, s0:s0 + strip].astype(F32)
            xe = jnp.concatenate([tail_ref[:, cols], xs_], axis=0)
            acc = cb_ref[0:1, cols] + cw_ref[SSD_CONV - 1:SSD_CONV, cols] * xs_
            for k in range(1, SSD_CONV):
                shifted = pltpu.roll(xe, k, 0)[SUBLANES:, :]
                acc = acc + cw_ref[SSD_CONV - 1 - k:SSD_CONV - k, cols] * shifted
            act_ref[:, cols] = _silu(acc)
            tail_ref[:, cols] = xs_[chunk - SUBLANES:, :]
        col0 += src.shape[1]

    dt_in = dt_ref[...] + dtb_ref[0:1, :]
    dt = jnp.maximum(dt_in, 0.0) + jnp.log(1.0 + jnp.exp(-jnp.abs(dt_in)))
    a = dt * aneg_ref[0:1, :]
    row = lax.broadcasted_iota(jnp.int32, (chunk, chunk), 0)
    col = lax.broadcasted_iota(jnp.int32, (chunk, chunk), 1)
    causal = row >= col
    cs = _dot_hi(jnp.where(causal, 1.0, 0.0), a)
    cs_t = cs.T
    cs_last = cs[chunk - 1:chunk, :]
    expand = exp_ref[...]
    dt_x = _dot_hi(dt, expand)
    ecs_x = _dot_hi(jnp.exp(cs), expand)
    dec_x = _dot_hi(jnp.exp(cs_last - cs), expand)
    sdec_x = _dot_hi(jnp.broadcast_to(jnp.exp(cs_last), (SUBLANES, LANES)), expand)[0:1, :]

    lane = lax.broadcasted_iota(jnp.int32, (chunk, LANES), 1)
    first_half = lane < head_dim
    for g in range(n_groups):
        gs = slice(g * gw, (g + 1) * gw)
        b_g = act_ref[:, width + g * d_state:width + (g + 1) * d_state].astype(BF16)
        c_g = act_ref[:, width + (n_groups + g) * d_state:width + (n_groups + g + 1) * d_state].astype(BF16)
        xs_g = act_ref[:, gs]
        xdt = xs_g * dt_x[:, gs]
        xdt_b = xdt.astype(BF16)
        cb = _nt_dot(c_g, b_g)
        st = state_ref[:, gs]
        y_g = jnp.dot(c_g, st.astype(BF16), preferred_element_type=F32) * ecs_x[:, gs] + dsk_ref[0:1, gs] * xs_g
        state_ref[:, gs] = st * sdec_x[:, gs] + _tn_dot(b_g, (xdt * dec_x[:, gs]).astype(BF16))
        for pr in range(heads_per_group // 2):
            xp = xdt_b[:, pr * LANES:(pr + 1) * LANES]
            halves = []
            for half in range(2):
                h = g * heads_per_group + 2 * pr + half
                seg = cs[:, h:h + 1] - cs_t[h:h + 1, :]
                m = (cb * jnp.where(causal, jnp.exp(seg), 0.0)).astype(BF16)
                halves.append(jnp.dot(m, xp, preferred_element_type=F32))
            y_pair = jnp.where(first_half, halves[0], halves[1])
            y_ref[:, g * gw + pr * LANES:g * gw + (pr + 1) * LANES] = y_pair + y_g[:, pr * LANES:(pr + 1) * LANES]

    yz = y_ref[...] * _silu(z_ref[...].astype(F32))
    ms = jnp.mean(yz * yz, axis=1, keepdims=True)
    o_ref[...] = (yz * lax.rsqrt(ms + RMS_EPS) * ng_ref[0:1, :]).astype(o_ref.dtype)


def _ssd(proj, dt_raw, cw, cb, dtb, aneg, dsk, ng, expand, *, batch, seq, width, n_groups, d_state,
         z_col, xbc_col):
    conv_dim = width + 2 * n_groups * d_state
    bc_w = n_groups * d_state
    nc = seq // SSD_CHUNK
    t = batch * seq
    kern = functools.partial(_ssd_kernel, n_groups=n_groups, head_dim=SSD_HEAD_DIM)
    b_col, c_col = xbc_col + width, xbc_col + width + bc_w
    assert z_col % width == 0 and xbc_col % width == 0 and b_col % bc_w == 0 and c_col % bc_w == 0
    const = lambda b, c: (0, 0)
    return pl.pallas_call(
        kern,
        grid=(batch, nc),
        in_specs=[
            pl.BlockSpec((SSD_CHUNK, width), lambda b, c: (b * nc + c, z_col // width)),
            pl.BlockSpec((SSD_CHUNK, width), lambda b, c: (b * nc + c, xbc_col // width)),
            pl.BlockSpec((SSD_CHUNK, bc_w), lambda b, c: (b * nc + c, b_col // bc_w)),
            pl.BlockSpec((SSD_CHUNK, bc_w), lambda b, c: (b * nc + c, c_col // bc_w)),
            pl.BlockSpec((SSD_CHUNK, LANES), lambda b, c: (b * nc + c, 0)),
            _resident((SUBLANES, conv_dim), const),
            _resident((SUBLANES, conv_dim), const),
            _resident((SUBLANES, LANES), const),
            _resident((SUBLANES, LANES), const),
            _resident((SUBLANES, width), const),
            _resident((SUBLANES, width), const),
            _resident((LANES, width), const),
        ],
        out_specs=pl.BlockSpec((SSD_CHUNK, width), lambda b, c: (b * nc + c, 0)),
        out_shape=jax.ShapeDtypeStruct((t, width), BF16),
        scratch_shapes=[
            pltpu.VMEM((SUBLANES, conv_dim), F32),
            pltpu.VMEM((d_state, width), F32),
            pltpu.VMEM((SSD_CHUNK, conv_dim), F32),
            pltpu.VMEM((SSD_CHUNK, width), F32),
        ],
        compiler_params=_cparams(("parallel", "arbitrary")),
        name="ssd0",
    )(proj, proj, proj, proj, dt_raw, cw, cb, dtb, aneg, dsk, ng, expand)


def _rope(v, cos, sin):
    half = ROPE_DIM // 2
    lane = lax.broadcasted_iota(jnp.int32, v.shape, 1)
    partner = jnp.where(lane < half, pltpu.roll(v, LANES - half, 1), pltpu.roll(v, half, 1))
    return v * cos + partner * sin


def _moba_kernel(q_ref, k_ref, v_ref, g_ref, cos_ref, sin_ref, o_ref, kr_ref, km_ref, *, n_blocks):
    qb = pl.program_id(2)
    blk = MOBA_BLOCK
    scale = ATT_HEAD_DIM ** -0.5

    @pl.when(qb == 0)
    def _():
        km_ref[...] = jnp.zeros_like(km_ref)
        for n in range(n_blocks):
            rows = slice(n * blk, (n + 1) * blk)
            kr = _rope(k_ref[rows, :].astype(F32), cos_ref[rows, :], sin_ref[rows, :])
            kr_ref[rows, :] = kr.astype(BF16)
            km_ref[n:n + 1, :] = jnp.mean(kr, axis=0, keepdims=True)

    q0 = pl.multiple_of(qb * blk, blk)
    qr = _rope(q_ref[...].astype(F32), cos_ref[pl.ds(q0, blk), :], sin_ref[pl.ds(q0, blk), :])
    qs = (qr * scale).astype(BF16)

    lane = lax.broadcasted_iota(jnp.int32, (blk, LANES), 1)
    gate = lax.dot_general(qr, km_ref[...], (((1,), (1,)), ((), ())), preferred_element_type=F32,
                           precision=HIGHEST)
    gate = jnp.where(lane < qb, gate, NEG_INF)
    rank = jnp.zeros((blk, LANES), jnp.int32)
    for n2 in range(n_blocks - 1):
        other = gate[:, n2:n2 + 1]
        beats = jnp.where(other > gate, 1, jnp.where(other == gate, jnp.where(lane > n2, 1, 0), 0))
        rank = rank + beats
    chosen = jnp.where(lane < qb, jnp.where(rank < MOBA_TOPK, 1.0, 0.0), 0.0)

    row = lax.broadcasted_iota(jnp.int32, (blk, blk), 0)
    col = lax.broadcasted_iota(jnp.int32, (blk, blk), 1)
    s = _nt_dot(qs, kr_ref[pl.ds(q0, blk), :])
    s = jnp.where(col <= row, s, NEG_INF)
    m0 = jnp.max(s, axis=1, keepdims=True)
    p = jnp.exp(s - m0)
    l0 = jnp.sum(p, axis=1, keepdims=True)
    acc0 = jnp.dot(p.astype(BF16), v_ref[pl.ds(q0, blk), :], preferred_element_type=F32)

    def past_block(n, carry):
        m, l, acc = carry
        n0 = pl.multiple_of(n * blk, blk)
        take = jnp.sum(jnp.where(lane == n, chosen, 0.0), axis=1, keepdims=True)
        sn = _nt_dot(qs, kr_ref[pl.ds(n0, blk), :])
        sn = jnp.where(take > 0.5, sn, NEG_INF)
        m_new = jnp.maximum(m, jnp.max(sn, axis=1, keepdims=True))
        alpha = jnp.exp(m - m_new)
        pn = jnp.exp(sn - m_new)
        l_new = alpha * l + jnp.sum(pn, axis=1, keepdims=True)
        acc_new = alpha * acc + jnp.dot(pn.astype(BF16), v_ref[pl.ds(n0, blk), :],
                                        preferred_element_type=F32)
        return m_new, l_new, acc_new

    _, l, acc = lax.fori_loop(0, qb, past_block, (m0, l0, acc0))
    o_ref[...] = (acc / l * _silu(g_ref[...].astype(F32))).astype(o_ref.dtype)


def _moba(proj, cos_t, sin_t, *, batch, seq, heads, q_col, k_col, v_col, g_col):
    nb = seq // MOBA_BLOCK
    hd = ATT_HEAD_DIM
    t = batch * seq
    qc, kc, vc, gc = (c // hd for c in (q_col, k_col, v_col, g_col))
    kern = functools.partial(_moba_kernel, n_blocks=nb)
    return pl.pallas_call(
        kern,
        grid=(batch, heads, nb),
        in_specs=[
            pl.BlockSpec((MOBA_BLOCK, hd), lambda b, h, i: (b * nb + i, qc + h)),
            pl.BlockSpec((seq, hd), lambda b, h, i: (b, kc + h)),
            pl.BlockSpec((seq, hd), lambda b, h, i: (b, vc + h)),
            pl.BlockSpec((MOBA_BLOCK, hd), lambda b, h, i: (b * nb + i, gc + h)),
            _resident((seq, hd), lambda b, h, i: (0, 0)),
            _resident((seq, hd), lambda b, h, i: (0, 0)),
        ],
        out_specs=pl.BlockSpec((MOBA_BLOCK, hd), lambda b, h, i: (b * nb + i, h)),
        out_shape=jax.ShapeDtypeStruct((t, heads * hd), BF16),
        scratch_shapes=[
            pltpu.VMEM((seq, hd), BF16),
            pltpu.VMEM((LANES, hd), F32),
        ],
        compiler_params=_cparams(("parallel", "parallel", "arbitrary")),
        name="moba0",
    )(proj, proj, proj, proj, cos_t, sin_t)


def _out0_kernel(ya_ref, yb_ref, x_ref, w_ref, g_ref, b_ref, o_ref, *, alpha):
    ka = ya_ref.shape[1]
    h = jnp.dot(ya_ref[...], w_ref[0:ka, :], preferred_element_type=F32)
    h = h + jnp.dot(yb_ref[...], w_ref[ka:, :], preferred_element_type=F32)
    v = alpha * x_ref[...] + h
    mu = jnp.mean(v, axis=1, keepdims=True)
    vc = v - mu
    var = jnp.mean(vc * vc, axis=1, keepdims=True)
    o_ref[...] = vc * lax.rsqrt(var + LN_EPS) * g_ref[0:1, :] + b_ref[0:1, :]


def _out0(ya, yb, x2, w, g, b, *, alpha, tm):
    t, d = x2.shape
    ka, kb = ya.shape[1], yb.shape[1]
    return pl.pallas_call(
        functools.partial(_out0_kernel, alpha=alpha),
        grid=(t // tm,),
        in_specs=[
            pl.BlockSpec((tm, ka), lambda i: (i, 0)),
            pl.BlockSpec((tm, kb), lambda i: (i, 0)),
            pl.BlockSpec((tm, d), lambda i: (i, 0)),
            _resident((ka + kb, d), lambda i: (0, 0)),
            _resident((SUBLANES, d), lambda i: (0, 0)),
            _resident((SUBLANES, d), lambda i: (0, 0)),
        ],
        out_specs=pl.BlockSpec((tm, d), lambda i: (i, 0)),
        out_shape=jax.ShapeDtypeStruct((t, d), F32),
        compiler_params=_cparams(("parallel",)),
        name="out0",
    )(ya, yb, x2, w, g, b)


def _inproj1_kernel(x_ref, w_ref, o_ref):
    o_ref[0] = jnp.dot(w_ref[...], x_ref[0].astype(BF16), preferred_element_type=F32).astype(o_ref.dtype)


def _inproj1(xt, wt):
    l, d, r = xt.shape
    n = wt.shape[0]
    return pl.pallas_call(
        _inproj1_kernel,
        grid=(l,),
        in_specs=[pl.BlockSpec((1, d, r), lambda i: (i, 0, 0)), _resident((n, d), lambda i: (0, 0))],
        out_specs=pl.BlockSpec((1, n, r), lambda i: (i, 0, 0)),
        out_shape=jax.ShapeDtypeStruct((l, n, r), BF16),
        compiler_params=_cparams(("parallel",)),
        name="inproj1",
    )(xt, wt)


def _cpow(lr, li, d, nbits):
    pr = jnp.ones(d.shape, F32)
    pi = jnp.zeros(d.shape, F32)
    br, bi = lr, li
    for bit in range(nbits):
        on = ((d >> bit) & 1) == 1
        nr = pr * br - pi * bi
        ni = pr * bi + pi * br
        pr = jnp.where(on, nr, pr)
        pi = jnp.where(on, ni, pi)
        if bit + 1 < nbits:
            br, bi = br * br - bi * bi, 2.0 * br * bi
    return pr, pi


def _gelu_tanh(v):
    return 0.5 * v * (1.0 + jnp.tanh(math.sqrt(2.0 / math.pi) * (v + 0.044715 * (v * v * v))))


def _s5_kernel(u_ref, lamc_ref, lamr_ref, bb_ref, cc_ref, ca_ref, cbm_ref, dsk_ref, o_ref, tz_ref,
               *, chunk, chunks_per_seq):
    m = S5_GROUP
    n = S5_STATE
    lm = chunk * m
    r = u_ref.shape[2]
    nbits = chunk.bit_length()
    u = u_ref[...].reshape(lm, r)

    lane = lax.broadcasted_iota(jnp.int32, (n, lm), 1)
    d_in = (chunk - 1) - lane // m
    lam_cr = jnp.concatenate([lamc_ref[0, 0]] * (lm // LANES), axis=1)
    lam_ci = jnp.concatenate([lamc_ref[0, 1]] * (lm // LANES), axis=1)
    pr, pi = _cpow(lam_cr, lam_ci, d_in, nbits)
    bbr = jnp.concatenate([bb_ref[0, 0]] * (lm // LANES), axis=1)
    bbi = jnp.concatenate([bb_ref[0, 1]] * (lm // LANES), axis=1)
    bl = jnp.concatenate([pr * bbr - pi * bbi, pr * bbi + pi * bbr], axis=0)

    krev = _dot_hi(cc_ref[0], bl)

    col_s = lax.broadcasted_iota(jnp.int32, (m, lm), 1) // m
    for t in range(chunk):
        rolled = pltpu.roll(krev, ((t + 1) * m) % lm, 1)
        tz_ref[t * m:(t + 1) * m, :] = jnp.where(col_s <= t, rolled, 0.0).astype(BF16)

    y = jnp.dot(tz_ref[...], u, preferred_element_type=F32)

    e = jnp.dot(bl.astype(BF16), u, preferred_element_type=F32)
    lane_r = lax.broadcasted_iota(jnp.int32, (n, r), 1) % chunks_per_seq
    hr = jnp.where(lane_r >= 1, pltpu.roll(e[:n], 1, 1), 0.0)
    hi = jnp.where(lane_r >= 1, pltpu.roll(e[n:], 1, 1), 0.0)
    d_l = jnp.full((n, LANES), chunk, jnp.int32)
    mr, mi = _cpow(lamc_ref[0, 0], lamc_ref[0, 1], d_l, nbits)
    mr = jnp.concatenate([mr] * (r // LANES), axis=1) if r > LANES else mr[:, :r]
    mi = jnp.concatenate([mi] * (r // LANES), axis=1) if r > LANES else mi[:, :r]
    step = 1
    while step < chunks_per_seq:
        sr = jnp.where(lane_r >= step, pltpu.roll(hr, step, 1), 0.0)
        si = jnp.where(lane_r >= step, pltpu.roll(hi, step, 1), 0.0)
        hr, hi = hr + mr * sr - mi * si, hi + mr * si + mi * sr
        mr, mi = mr * mr - mi * mi, 2.0 * mr * mi
        step *= 2
    h_in = jnp.concatenate([hr, hi], axis=0).astype(BF16)

    rowt = lax.broadcasted_iota(jnp.int32, (lm, 2 * n), 0) // m + 1
    lam_rr = jnp.broadcast_to(lamr_ref[0, 0:1, :], (lm, 2 * n))
    lam_ri = jnp.broadcast_to(lamr_ref[0, 1:2, :], (lm, 2 * n))
    qr, qi = _cpow(lam_rr, lam_ri, rowt, nbits)
    ca = jnp.concatenate([ca_ref[0]] * chunk, axis=0)
    cbm = jnp.concatenate([cbm_ref[0]] * chunk, axis=0)
    cl = (ca * qr + cbm * qi).astype(BF16)
    y = y + jnp.dot(cl, h_in, preferred_element_type=F32)

    dsk = jnp.concatenate([dsk_ref[0]] * chunk, axis=0)
    dsk = jnp.concatenate([dsk] * (r // LANES), axis=1) if r > LANES else dsk[:, :r]
    y = y + dsk * u.astype(F32)
    o_ref[...] = _gelu_tanh(y).astype(o_ref.dtype).reshape(chunk, m, r)


def _s5(ug, lamc, lamr, bb, cc, ca, cbm, dsk, *, chunks_per_seq):
    chunk, _, r = ug.shape
    g = lamc.shape[0]
    m, n = S5_GROUP, S5_STATE
    lm = chunk * m
    kern = functools.partial(_s5_kernel, chunk=chunk, chunks_per_seq=chunks_per_seq)
    p4 = lambda i: (i, 0, 0, 0)
    p3 = lambda i: (i, 0, 0)
    return pl.pallas_call(
        kern,
        grid=(g,),
        in_specs=[
            pl.BlockSpec((chunk, m, r), lambda i: (0, i, 0)),
            pl.BlockSpec((1, 2, n, LANES), p4),
            pl.BlockSpec((1, SUBLANES, 2 * n), p3),
            pl.BlockSpec((1, 2, n, LANES), p4),
            pl.BlockSpec((1, m, 2 * n), p3),
            pl.BlockSpec((1, m, 2 * n), p3),
            pl.BlockSpec((1, m, 2 * n), p3),
            pl.BlockSpec((1, m, LANES), p3),
        ],
        out_specs=pl.BlockSpec((chunk, m, r), lambda i: (0, i, 0)),
        out_shape=jax.ShapeDtypeStruct((chunk, g * m, r), BF16),
        scratch_shapes=[pltpu.VMEM((lm, lm), BF16)],
        compiler_params=_cparams(("parallel",)),
        name="s5scan",
    )(ug, lamc, lamr, bb, cc, ca, cbm, dsk)


def _out1_kernel(y_ref, gate_ref, x_ref, wg_ref, wo_ref, g_ref, b_ref, o_ref, *, alpha):
    w = y_ref.shape[1]
    glu = jnp.dot(wg_ref[...], y_ref[0], preferred_element_type=F32)
    ga, gb = glu[:w], glu[w:]
    v = ga * (1.0 / (1.0 + jnp.exp(-gb))) * _silu(gate_ref[0].astype(F32))
    h = jnp.dot(wo_ref[...], v.astype(BF16), preferred_element_type=F32)
    s = alpha * x_ref[0] + h
    mu = jnp.mean(s, axis=0, keepdims=True)
    sc = s - mu
    var = jnp.mean(sc * sc, axis=0, keepdims=True)
    o_ref[0] = sc * lax.rsqrt(var + LN_EPS) * g_ref[...] + b_ref[...]


def _out1(y3, ug, xt, wgt, wot, g_col, b_col, *, alpha):
    l, w, r = y3.shape
    d = xt.shape[1]
    gate_blk = 1
    return pl.pallas_call(
        functools.partial(_out1_kernel, alpha=alpha),
        grid=(l,),
        in_specs=[
            pl.BlockSpec((1, w, r), lambda i: (i, 0, 0)),
            pl.BlockSpec((1, w, r), lambda i: (i, gate_blk, 0)),
            pl.BlockSpec((1, d, r), lambda i: (i, 0, 0)),
            _resident((2 * w, w), lambda i: (0, 0)),
            _resident((d, w), lambda i: (0, 0)),
            _resident((d, 1), lambda i: (0, 0)),
            _resident((d, 1), lambda i: (0, 0)),
        ],
        out_specs=pl.BlockSpec((1, d, r), lambda i: (i, 0, 0)),
        out_shape=jax.ShapeDtypeStruct((l, d, r), F32),
        compiler_params=_cparams(("parallel",)),
        name="out1",
    )(y3, ug, xt, wgt, wot, g_col, b_col)


def _pad_rows(v, rows=SUBLANES):
    v = jnp.atleast_2d(v.astype(F32))
    return jnp.pad(v, ((0, rows - v.shape[0]), (0, 0)))


def _pad_lanes(v, lanes=LANES):
    return jnp.pad(v, [(0, 0)] * (v.ndim - 1) + [(0, lanes - v.shape[-1])])


def _layer0(x2, batch, seq, in_w, conv_w, conv_b, dt_bias, a_log, d_skip, norm_g, out_w, ln_g, ln_b, alpha):
    d = x2.shape[1]
    ssd_heads = dt_bias.shape[0]
    width = ssd_heads * SSD_HEAD_DIM
    conv_dim = conv_w.shape[1]
    d_state = (conv_dim - width) // (2 * SSD_GROUPS)
    att_width = (in_w.shape[1] - width - conv_dim - ssd_heads) // 4
    att_heads = att_width // ATT_HEAD_DIM

    z_end = width
    xbc_end = z_end + conv_dim
    dt_end = xbc_end + ssd_heads
    w_main = jnp.concatenate([in_w[:, :xbc_end], in_w[:, dt_end:]], axis=1).astype(BF16)
    w_dt = _pad_lanes(in_w[:, xbc_end:dt_end]).astype(BF16)
    z_col, xbc_col = 0, z_end
    q_col = xbc_end
    k_col, v_col, g_col = q_col + att_width, q_col + 2 * att_width, q_col + 3 * att_width

    proj, dt_raw = _inproj(x2, w_main, w_dt, tm=1024, tn=1024)

    expand = (jnp.arange(LANES)[:, None] == (jnp.arange(width)[None, :] // SSD_HEAD_DIM)).astype(F32)
    ya = _ssd(proj, dt_raw, _pad_rows(conv_w), _pad_rows(conv_b), _pad_rows(_pad_lanes(dt_bias.astype(F32))),
              _pad_rows(_pad_lanes(-jnp.exp(a_log.astype(F32)))),
              _pad_rows(jnp.repeat(d_skip.astype(F32), SSD_HEAD_DIM)), _pad_rows(norm_g), expand,
              batch=batch, seq=seq, width=width, n_groups=SSD_GROUPS, d_state=d_state,
              z_col=z_col, xbc_col=xbc_col)

    half = ROPE_DIM // 2
    inv_freq = ROPE_THETA ** (-(jnp.arange(half, dtype=F32) * 2.0 / ROPE_DIM))
    ang = jnp.arange(seq, dtype=F32)[:, None] * inv_freq[None, :]
    ones = jnp.ones((seq, ATT_HEAD_DIM - ROPE_DIM), F32)
    cos_t = jnp.concatenate([jnp.cos(ang), jnp.cos(ang), ones], axis=1)
    sin_t = jnp.concatenate([-jnp.sin(ang), jnp.sin(ang), 0.0 * ones], axis=1)
    yb = _moba(proj, cos_t, sin_t, batch=batch, seq=seq, heads=att_heads,
               q_col=q_col, k_col=k_col, v_col=v_col, g_col=g_col)

    return _out0(ya, yb, x2, out_w.astype(BF16), _pad_rows(ln_g), _pad_rows(ln_b), alpha=alpha, tm=512)


def _layer1(x2, batch, seq, in_w, lam_re, lam_im, log_dt, b_re, b_im, c_re, c_im, d_skip, glu_w, out_w,
            ln_g, ln_b, alpha):
    t, d = x2.shape
    chunk = S5_L
    cps = seq // chunk
    r = batch * cps
    groups, n = lam_re.shape
    m = S5_GROUP
    w = groups * m

    lam = lax.complex(lam_re.astype(F32), lam_im.astype(F32))
    dt = jnp.exp(log_dt.astype(F32))[:, None]
    lam_bar = jnp.exp(lam * dt)
    b_bar = ((lam_bar - 1.0) / lam)[..., None] * lax.complex(b_re.astype(F32), b_im.astype(F32))
    lbr, lbi = jnp.real(lam_bar), jnp.imag(lam_bar)
    lamc = jnp.broadcast_to(jnp.stack([lbr, lbi], axis=1)[..., None], (groups, 2, n, LANES))
    lamr = jnp.stack([jnp.concatenate([lbr, lbr], -1), jnp.concatenate([lbi, lbi], -1)], axis=1)
    lamr = jnp.pad(lamr, ((0, 0), (0, SUBLANES - 2), (0, 0)))
    bb = jnp.tile(jnp.stack([jnp.real(b_bar), jnp.imag(b_bar)], axis=1), (1, 1, 1, LANES // m))
    cr, ci = c_re.astype(F32), c_im.astype(F32)
    cc = jnp.concatenate([cr, -ci], axis=-1)
    cbm = jnp.concatenate([-ci, -cr], axis=-1)
    dsk = jnp.broadcast_to(d_skip.astype(F32).reshape(groups, m, 1), (groups, m, LANES))

    xt = x2.reshape(r, chunk, d).transpose(1, 2, 0)
    ug = _inproj1(xt, in_w.T.astype(BF16))
    y3 = _s5(ug, lamc, lamr, bb, cc, cc, cbm, dsk, chunks_per_seq=cps)
    o3 = _out1(y3, ug, xt, glu_w.T.astype(BF16), out_w.T.astype(BF16),
               ln_g.astype(F32)[:, None], ln_b.astype(F32)[:, None], alpha=alpha)
    return o3.transpose(2, 0, 1).reshape(t, d)


def kernel(x, in0_w, conv_w, conv_b, dt_bias, a_log, ssd_d, ssd_norm_g, out0_w, in1_w, s5_lam_re, s5_lam_im,
           s5_log_dt, s5_b_re, s5_b_im, s5_c_re, s5_c_im, s5_d, glu_w, out1_w, ln_g, ln_b):
    batch, seq, d = x.shape
    depth = ln_g.shape[0]
    alpha = (2 * depth) ** 0.25
    x2 = x.reshape(batch * seq, d)
    for layer in range(depth):
        i = layer // 2
        if layer % 2 == 0:
            x2 = _layer0(x2, batch, seq, in0_w[i], conv_w[i], conv_b[i], dt_bias[i], a_log[i], ssd_d[i],
                         ssd_norm_g[i], out0_w[i], ln_g[layer], ln_b[layer], alpha)
        else:
            x2 = _layer1(x2, batch, seq, in1_w[i], s5_lam_re[i], s5_lam_im[i], s5_log_dt[i], s5_b_re[i],
                         s5_b_im[i], s5_c_re[i], s5_c_im[i], s5_d[i], glu_w[i], out1_w[i],
                         ln_g[layer], ln_b[layer], alpha)
    return x2.reshape(batch, seq, d).astype(x.dtype)
```

```python
import functools
import math

import jax
import jax.numpy as jnp
from jax import lax
from jax.experimental import pallas as pl
from jax.experimental.pallas import tpu as pltpu

F32 = jnp.float32
BF16 = jnp.bfloat16
HIGHEST = lax.Precision.HIGHEST

SSD_HEAD_DIM = 64
SSD_GROUPS = 4
SSD_STATE = 128
SSD_CONV = 4
SSD_CHUNK = 256
ATT_HEAD_DIM = 128
MOBA_BLOCK = 256
MOBA_TOPK = 3
ROPE_THETA = 500000.0
ROPE_DIM = ATT_HEAD_DIM // 4
S5_GROUP = 16
S5_STATE = 64
LN_EPS = 1e-5
RMS_EPS = 1e-5
NEG_INF = -1e30

LANES = 128
SUBLANES = 8
VMEM_LIMIT = 56 * 1024 * 1024

S5_L = 64


def _cparams(sem):
    return pltpu.CompilerParams(dimension_semantics=sem, vmem_limit_bytes=VMEM_LIMIT)


def _resident(shape, index_map):
    return pl.BlockSpec(shape, index_map, pipeline_mode=pl.Buffered(1))


def _silu(v):
    return v * (1.0 / (1.0 + jnp.exp(-v)))


def _nt_dot(a, b):
    return lax.dot_general(a, b, (((1,), (1,)), ((), ())), preferred_element_type=F32)


def _tn_dot(a, b):
    return lax.dot_general(a, b, (((0,), (0,)), ((), ())), preferred_element_type=F32)


def _dot_hi(a, b):
    return jnp.dot(a, b, preferred_element_type=F32, precision=HIGHEST)


def _inproj_kernel(x_ref, w_ref, wdt_ref, o_ref, dt_ref, xb_ref):
    @pl.when(pl.program_id(1) == 0)
    def _():
        xb = x_ref[...].astype(BF16)
        xb_ref[...] = xb
        dt_ref[...] = jnp.dot(xb, wdt_ref[...], preferred_element_type=F32)

    o_ref[...] = jnp.dot(xb_ref[...], w_ref[...], preferred_element_type=F32).astype(o_ref.dtype)


def _inproj(x2, w_main, w_dt, tm, tn):
    t, d = x2.shape
    n = w_main.shape[1]
    return pl.pallas_call(
        _inproj_kernel,
        grid=(t // tm, n // tn),
        in_specs=[
            pl.BlockSpec((tm, d), lambda i, j: (i, 0)),
            pl.BlockSpec((d, tn), lambda i, j: (0, j)),
            _resident((d, LANES), lambda i, j: (0, 0)),
        ],
        out_specs=[
            pl.BlockSpec((tm, tn), lambda i, j: (i, j)),
            pl.BlockSpec((tm, LANES), lambda i, j: (i, 0)),
        ],
        out_shape=[jax.ShapeDtypeStruct((t, n), BF16), jax.ShapeDtypeStruct((t, LANES), F32)],
        scratch_shapes=[pltpu.VMEM((tm, d), BF16)],
        compiler_params=_cparams(("parallel", "arbitrary")),
        name="inproj0",
    )(x2, w_main, w_dt)


def _ssd_kernel(z_ref, xs_ref, bm_ref, cm_ref, dt_ref, cw_ref, cb_ref, dtb_ref, aneg_ref, dsk_ref, ng_ref,
                exp_ref, o_ref, tail_ref, state_ref, act_ref, y_ref, *, n_groups, head_dim):
    c = pl.program_id(1)
    chunk, width = z_ref.shape
    d_state = bm_ref.shape[1] // n_groups
    heads_per_group = width // head_dim // n_groups
    gw = heads_per_group * head_dim

    @pl.when(c == 0)
    def _():
        tail_ref[...] = jnp.zeros_like(tail_ref)
        state_ref[...] = jnp.zeros_like(state_ref)

    strip = 512
    col0 = 0
    for src in (xs_ref, bm_ref, cm_ref):
        for s0 in range(0, src.shape[1], strip):
            cols = slice(col0 + s0, col0 + s0 + strip)
            xs_ = src[:, s0:s0 + strip].astype(F32)
            xe = jnp.concatenate([tail_ref[:, cols], xs_], axis=0)
            acc = cb_ref[0:1, cols] + cw_ref[SSD_CONV - 1:SSD_CONV, cols] * xs_
            for k in range(1, SSD_CONV):
                shifted = pltpu.roll(xe, k, 0)[SUBLANES:, :]
                acc = acc + cw_ref[SSD_CONV - 1 - k:SSD_CONV - k, cols] * shifted
            act_ref[:, cols] = _silu(acc)
            tail_ref[:, cols] = xs_[chunk - SUBLANES:, :]
        col0 += src.shape[1]

    dt_in = dt_ref[...] + dtb_ref[0:1, :]
    dt = jnp.maximum(dt_in, 0.0) + jnp.log(1.0 + jnp.exp(-jnp.abs(dt_in)))
    a = dt * aneg_ref[0:1, :]
    row = lax.broadcasted_iota(jnp.int32, (chunk, chunk), 0)
    col = lax.broadcasted_iota(jnp.int32, (chunk, chunk), 1)
    causal = row >= col
    cs = _dot_hi(jnp.where(causal, 1.0, 0.0), a)
    cs_t = cs.T
    cs_last = cs[chunk - 1:chunk, :]
    expand = exp_ref[...]
    dt_x = _dot_hi(dt, expand)
    ecs_x = _dot_hi(jnp.exp(cs), expand)
    dec_x = _dot_hi(jnp.exp(cs_last - cs), expand)
    sdec_x = _dot_hi(jnp.broadcast_to(jnp.exp(cs_last), (SUBLANES, LANES)), expand)[0:1, :]

    lane = lax.broadcasted_iota(jnp.int32, (chunk, LANES), 1)
    first_half = lane < head_dim
    for g in range(n_groups):
        gs = slice(g * gw, (g + 1) * gw)
        b_g = act_ref[:, width + g * d_state:width + (g + 1) * d_state].astype(BF16)
        c_g = act_ref[:, width + (n_groups + g) * d_state:width + (n_groups + g + 1) * d_state].astype(BF16)
        xs_g = act_ref[:, gs]
        xdt = xs_g * dt_x[:, gs]
        xdt_b = xdt.astype(BF16)
        cb = _nt_dot(c_g, b_g)
        st = state_ref[:, gs]
        y_g = jnp.dot(c_g, st.astype(BF16), preferred_element_type=F32) * ecs_x[:, gs] + dsk_ref[0:1, gs] * xs_g
        state_ref[:, gs] = st * sdec_x[:, gs] + _tn_dot(b_g, (xdt * dec_x[:, gs]).astype(BF16))
        for pr in range(heads_per_group // 2):
            xp = xdt_b[:, pr * LANES:(pr + 1) * LANES]
            halves = []
            for half in range(2):
                h = g * heads_per_group + 2 * pr + half
                seg = cs[:, h:h + 1] - cs_t[h:h + 1, :]
                m = (cb * jnp.where(causal, jnp.exp(seg), 0.0)).astype(BF16)
                halves.append(jnp.dot(m, xp, preferred_element_type=F32))
            y_pair = jnp.where(first_half, halves[0], halves[1])
            y_ref[:, g * gw + pr * LANES:g * gw + (pr + 1) * LANES] = y_pair + y_g[:, pr * LANES:(pr + 1) * LANES]

    yz = y_ref[...] * _silu(z_ref[...].astype(F32))
    ms = jnp.mean(yz * yz, axis=1, keepdims=True)
    o_ref[...] = (yz * lax.rsqrt(ms + RMS_EPS) * ng_ref[0:1, :]).astype(o_ref.dtype)


def _ssd(proj, dt_raw, cw, cb, dtb, aneg, dsk, ng, expand, *, batch, seq, width, n_groups, d_state,
         z_col, xbc_col):
    conv_dim = width + 2 * n_groups * d_state
    bc_w = n_groups * d_state
    nc = seq // SSD_CHUNK
    t = batch * seq
    kern = functools.partial(_ssd_kernel, n_groups=n_groups, head_dim=SSD_HEAD_DIM)
    b_col, c_col = xbc_col + width, xbc_col + width + bc_w
    assert z_col % width == 0 and xbc_col % width == 0 and b_col % bc_w == 0 and c_col % bc_w == 0
    const = lambda b, c: (0, 0)
    return pl.pallas_call(
        kern,
        grid=(batch, nc),
        in_specs=[
            pl.BlockSpec((SSD_CHUNK, width), lambda b, c: (b * nc + c, z_col // width)),
            pl.BlockSpec((SSD_CHUNK, width), lambda b, c: (b * nc + c, xbc_col // width)),
            pl.BlockSpec((SSD_CHUNK, bc_w), lambda b, c: (b * nc + c, b_col // bc_w)),
            pl.BlockSpec((SSD_CHUNK, bc_w), lambda b, c: (b * nc + c, c_col // bc_w)),
            pl.BlockSpec((SSD_CHUNK, LANES), lambda b, c: (b * nc + c, 0)),
            _resident((SUBLANES, conv_dim), const),
            _resident((SUBLANES, conv_dim), const),
            _resident((SUBLANES, LANES), const),
            _resident((SUBLANES, LANES), const),
            _resident((SUBLANES, width), const),
            _resident((SUBLANES, width), const),
            _resident((LANES, width), const),
        ],
        out_specs=pl.BlockSpec((SSD_CHUNK, width), lambda b, c: (b * nc + c, 0)),
        out_shape=jax.ShapeDtypeStruct((t, width), BF16),
        scratch_shapes=[
            pltpu.VMEM((SUBLANES, conv_dim), F32),
            pltpu.VMEM((d_state, width), F32),
            pltpu.VMEM((SSD_CHUNK, conv_dim), F32),
            pltpu.VMEM((SSD_CHUNK, width), F32),
        ],
        compiler_params=_cparams(("parallel", "arbitrary")),
        name="ssd0",
    )(proj, proj, proj, proj, dt_raw, cw, cb, dtb, aneg, dsk, ng, expand)


def _rope(v, cos, sin):
    half = ROPE_DIM // 2
    lane = lax.broadcasted_iota(jnp.int32, v.shape, 1)
    partner = jnp.where(lane < half, pltpu.roll(v, LANES - half, 1), pltpu.roll(v, half, 1))
    return v * cos + partner * sin


def _moba_kernel(q_ref, k_ref, v_ref, g_ref, cos_ref, sin_ref, o_ref, kr_ref, vt_ref, km_ref, ch_ref,
                 sa_ref, sb_ref, *, n_blocks, heads_per_step):
    qb = pl.program_id(2)
    blk = MOBA_BLOCK
    hd = ATT_HEAD_DIM
    scale = hd ** -0.5 * math.log2(math.e)
    heads = range(heads_per_step)

    @pl.when(qb == 0)
    def _():
        for n in range(n_blocks):
            rows = slice(n * blk, (n + 1) * blk)
            for j in heads:
                cols = slice(j * hd, (j + 1) * hd)
                kr = _rope(k_ref[rows, cols].astype(F32), cos_ref[rows, :], sin_ref[rows, :])
                kr_ref[j, rows, :] = kr.astype(BF16)
                km_ref[j, n:n + 1, :] = jnp.mean(kr, axis=0, keepdims=True)
                vt_ref[j, 0:hd, rows] = v_ref[rows, cols].astype(F32).T.astype(BF16)
                vt_ref[j, hd:, rows] = jnp.ones((vt_ref.shape[1] - hd, blk), BF16)

    q0 = pl.multiple_of(qb * blk, blk)
    cos_q = cos_ref[pl.ds(q0, blk), :]
    sin_q = sin_ref[pl.ds(q0, blk), :]
    blk_id = lax.broadcasted_iota(jnp.int32, (n_blocks, blk), 0)
    key_i = lax.broadcasted_iota(jnp.int32, (blk, blk), 0)
    qry_i = lax.broadcasted_iota(jnp.int32, (blk, blk), 1)

    qs_t, init = [], []
    for j in heads:
        qr_t = _rope(q_ref[:, j * hd:(j + 1) * hd].astype(F32), cos_q, sin_q).T
        qs_t.append((qr_t * scale).astype(BF16))

        gate = jnp.where(blk_id < qb, _dot_hi(km_ref[j], qr_t), NEG_INF)
        rank = jnp.zeros((n_blocks, blk), jnp.int32)
        for n2 in range(n_blocks - 1):
            other = gate[n2:n2 + 1, :]
            rank = rank + jnp.where(other > gate, 1,
                                    jnp.where(other == gate, jnp.where(blk_id > n2, 1, 0), 0))
        ch_ref[j] = jnp.where(blk_id < qb, jnp.where(rank < MOBA_TOPK, 1.0, 0.0), 0.0)

        s = jnp.dot(kr_ref[j, pl.ds(q0, blk), :], qs_t[j], preferred_element_type=F32)
        s = jnp.where(key_i <= qry_i, s, NEG_INF)
        m0 = jnp.max(s, axis=0, keepdims=True)
        p = jnp.exp2(s - m0).astype(BF16)
        init.append((m0, jnp.dot(vt_ref[j, :, pl.ds(q0, blk)], p, preferred_element_type=F32)))

    def stage(n, ref):
        n0 = pl.multiple_of(jnp.minimum(n, n_blocks - 1) * blk, blk)
        for j in heads:
            ref[j] = jnp.dot(kr_ref[j, pl.ds(n0, blk), :], qs_t[j], preferred_element_type=F32)

    def consume(n, ref, carry):
        n0 = pl.multiple_of(n * blk, blk)
        out = []
        for j in heads:
            m, acc = carry[j]
            taken = ch_ref[j, pl.ds(n, 1), :] > 0.5
            m_new = jnp.maximum(m, jnp.where(taken, jnp.max(ref[j], axis=0, keepdims=True), NEG_INF))
            alpha = jnp.exp2(m - m_new)
            p = jnp.exp2(ref[j] - jnp.where(taken, m_new, -NEG_INF)).astype(BF16)
            out.append((m_new, alpha * acc + jnp.dot(vt_ref[j, :, pl.ds(n0, blk)], p,
                                                     preferred_element_type=F32)))
        return tuple(out)

    def two_blocks(i, carry):
        stage(2 * i + 1, sb_ref)
        carry = consume(2 * i, sa_ref, carry)
        stage(2 * i + 2, sa_ref)
        return consume(2 * i + 1, sb_ref, carry)

    stage(0, sa_ref)
    final = lax.fori_loop(0, (qb + 1) // 2, two_blocks, tuple(init))
    for j in heads:
        _, acc = final[j]
        cols = slice(j * hd, (j + 1) * hd)
        out_t = acc[0:hd, :] / acc[hd:hd + 1, :]
        o_ref[:, cols] = (out_t.T * _silu(g_ref[:, cols].astype(F32))).astype(o_ref.dtype)


def _moba(proj, cos_t, sin_t, *, batch, seq, heads, q_col, k_col, v_col, g_col, heads_per_step):
    nb = seq // MOBA_BLOCK
    hd = ATT_HEAD_DIM
    hw = hd * heads_per_step
    t = batch * seq
    assert heads % heads_per_step == 0 and all(c % hw == 0 for c in (q_col, k_col, v_col, g_col))
    qc, kc, vc, gc = (c // hw for c in (q_col, k_col, v_col, g_col))
    kern = functools.partial(_moba_kernel, n_blocks=nb, heads_per_step=heads_per_step)
    return pl.pallas_call(
        kern,
        grid=(batch, heads // heads_per_step, nb),
        in_specs=[
            pl.BlockSpec((MOBA_BLOCK, hw), lambda b, h, i: (b * nb + i, qc + h)),
            pl.BlockSpec((seq, hw), lambda b, h, i: (b, kc + h)),
            pl.BlockSpec((seq, hw), lambda b, h, i: (b, vc + h)),
            pl.BlockSpec((MOBA_BLOCK, hw), lambda b, h, i: (b * nb + i, gc + h)),
            _resident((seq, hd), lambda b, h, i: (0, 0)),
            _resident((seq, hd), lambda b, h, i: (0, 0)),
        ],
        out_specs=pl.BlockSpec((MOBA_BLOCK, hw), lambda b, h, i: (b * nb + i, h)),
        out_shape=jax.ShapeDtypeStruct((t, heads * hd), BF16),
        scratch_shapes=[
            pltpu.VMEM((heads_per_step, seq, hd), BF16),
            pltpu.VMEM((heads_per_step, hd + 16, seq), BF16),
            pltpu.VMEM((heads_per_step, nb, hd), F32),
            pltpu.VMEM((heads_per_step, nb, MOBA_BLOCK), F32),
            pltpu.VMEM((heads_per_step, MOBA_BLOCK, MOBA_BLOCK), F32),
            pltpu.VMEM((heads_per_step, MOBA_BLOCK, MOBA_BLOCK), F32),
        ],
        compiler_params=_cparams(("parallel", "parallel", "arbitrary")),
        name="moba0",
    )(proj, proj, proj, proj, cos_t, sin_t)


def _out0_kernel(ya_ref, yb_ref, x_ref, w_ref, g_ref, b_ref, o_ref, *, alpha):
    ka = ya_ref.shape[1]
    h = jnp.dot(ya_ref[...], w_ref[0:ka, :], preferred_element_type=F32)
    h = h + jnp.dot(yb_ref[...], w_ref[ka:, :], preferred_element_type=F32)
    v = alpha * x_ref[...] + h
    mu = jnp.mean(v, axis=1, keepdims=True)
    vc = v - mu
    var = jnp.mean(vc * vc, axis=1, keepdims=True)
    o_ref[...] = vc * lax.rsqrt(var + LN_EPS) * g_ref[0:1, :] + b_ref[0:1, :]


def _out0(ya, yb, x2, w, g, b, *, alpha, tm):
    t, d = x2.shape
    ka, kb = ya.shape[1], yb.shape[1]
    return pl.pallas_call(
        functools.partial(_out0_kernel, alpha=alpha),
        grid=(t // tm,),
        in_specs=[
            pl.BlockSpec((tm, ka), lambda i: (i, 0)),
            pl.BlockSpec((tm, kb), lambda i: (i, 0)),
            pl.BlockSpec((tm, d), lambda i: (i, 0)),
            _resident((ka + kb, d), lambda i: (0, 0)),
            _resident((SUBLANES, d), lambda i: (0, 0)),
            _resident((SUBLANES, d), lambda i: (0, 0)),
        ],
        out_specs=pl.BlockSpec((tm, d), lambda i: (i, 0)),
        out_shape=jax.ShapeDtypeStruct((t, d), F32),
        compiler_params=_cparams(("parallel",)),
        name="out0",
    )(ya, yb, x2, w, g, b)


def _inproj1_kernel(x_ref, w_ref, o_ref):
    o_ref[0] = jnp.dot(w_ref[...], x_ref[0].astype(BF16), preferred_element_type=F32).astype(o_ref.dtype)


def _inproj1(xt, wt):
    l, d, r = xt.shape
    n = wt.shape[0]
    return pl.pallas_call(
        _inproj1_kernel,
        grid=(l,),
        in_specs=[pl.BlockSpec((1, d, r), lambda i: (i, 0, 0)), _resident((n, d), lambda i: (0, 0))],
        out_specs=pl.BlockSpec((1, n, r), lambda i: (i, 0, 0)),
        out_shape=jax.ShapeDtypeStruct((l, n, r), BF16),
        compiler_params=_cparams(("parallel",)),
        name="inproj1",
    )(xt, wt)


def _cpow(lr, li, d, nbits):
    pr = jnp.ones(d.shape, F32)
    pi = jnp.zeros(d.shape, F32)
    br, bi = lr, li
    for bit in range(nbits):
        on = ((d >> bit) & 1) == 1
        nr = pr * br - pi * bi
        ni = pr * bi + pi * br
        pr = jnp.where(on, nr, pr)
        pi = jnp.where(on, ni, pi)
        if bit + 1 < nbits:
            br, bi = br * br - bi * bi, 2.0 * br * bi
    return pr, pi


def _gelu_tanh(v):
    return 0.5 * v * (1.0 + jnp.tanh(math.sqrt(2.0 / math.pi) * (v + 0.044715 * (v * v * v))))


def _s5_kernel(u_ref, lamc_ref, lamr_ref, bb_ref, cc_ref, ca_ref, cbm_ref, dsk_ref, o_ref, tz_ref,
               *, chunk, chunks_per_seq):
    m = S5_GROUP
    n = S5_STATE
    lm = chunk * m
    r = u_ref.shape[2]
    nbits = chunk.bit_length()
    u = u_ref[...].reshape(lm, r)

    lane = lax.broadcasted_iota(jnp.int32, (n, lm), 1)
    d_in = (chunk - 1) - lane // m
    lam_cr = jnp.concatenate([lamc_ref[0, 0]] * (lm // LANES), axis=1)
    lam_ci = jnp.concatenate([lamc_ref[0, 1]] * (lm // LANES), axis=1)
    pr, pi = _cpow(lam_cr, lam_ci, d_in, nbits)
    bbr = jnp.concatenate([bb_ref[0, 0]] * (lm // LANES), axis=1)
    bbi = jnp.concatenate([bb_ref[0, 1]] * (lm // LANES), axis=1)
    bl = jnp.concatenate([pr * bbr - pi * bbi, pr * bbi + pi * bbr], axis=0)

    krev = _dot_hi(cc_ref[0], bl)

    col_s = lax.broadcasted_iota(jnp.int32, (m, lm), 1) // m
    for t in range(chunk):
        rolled = pltpu.roll(krev, ((t + 1) * m) % lm, 1)
        tz_ref[t * m:(t + 1) * m, :] = jnp.where(col_s <= t, rolled, 0.0).astype(BF16)

    y = jnp.dot(tz_ref[...], u, preferred_element_type=F32)

    e = jnp.dot(bl.astype(BF16), u, preferred_element_type=F32)
    lane_r = lax.broadcasted_iota(jnp.int32, (n, r), 1) % chunks_per_seq
    hr = jnp.where(lane_r >= 1, pltpu.roll(e[:n], 1, 1), 0.0)
    hi = jnp.where(lane_r >= 1, pltpu.roll(e[n:], 1, 1), 0.0)
    d_l = jnp.full((n, LANES), chunk, jnp.int32)
    mr, mi = _cpow(lamc_ref[0, 0], lamc_ref[0, 1], d_l, nbits)
    mr = jnp.concatenate([mr] * (r // LANES), axis=1) if r > LANES else mr[:, :r]
    mi = jnp.concatenate([mi] * (r // LANES), axis=1) if r > LANES else mi[:, :r]
    step = 1
    while step < chunks_per_seq:
        sr = jnp.where(lane_r >= step, pltpu.roll(hr, step, 1), 0.0)
        si = jnp.where(lane_r >= step, pltpu.roll(hi, step, 1), 0.0)
        hr, hi = hr + mr * sr - mi * si, hi + mr * si + mi * sr
        mr, mi = mr * mr - mi * mi, 2.0 * mr * mi
        step *= 2
    h_in = jnp.concatenate([hr, hi], axis=0).astype(BF16)

    rowt = lax.broadcasted_iota(jnp.int32, (lm, 2 * n), 0) // m + 1
    lam_rr = jnp.broadcast_to(lamr_ref[0, 0:1, :], (lm, 2 * n))
    lam_ri = jnp.broadcast_to(lamr_ref[0, 1:2, :], (lm, 2 * n))
    qr, qi = _cpow(lam_rr, lam_ri, rowt, nbits)
    ca = jnp.concatenate([ca_ref[0]] * chunk, axis=0)
    cbm = jnp.concatenate([cbm_ref[0]] * chunk, axis=0)
    cl = (ca * qr + cbm * qi).astype(BF16)
    y = y + jnp.dot(cl, h_in, preferred_element_type=F32)

    dsk = jnp.concatenate([dsk_ref[0]] * chunk, axis=0)
    dsk = jnp.concatenate([dsk] * (r // LANES), axis=1) if r > LANES else dsk[:, :r]
    y = y + dsk * u.astype(F32)
    o_ref[...] = _gelu_tanh(y).astype(o_ref.dtype).reshape(chunk, m, r)


def _s5(ug, lamc, lamr, bb, cc, ca, cbm, dsk, *, chunks_per_seq):
    chunk, _, r = ug.shape
    g = lamc.shape[0]
    m, n = S5_GROUP, S5_STATE
    lm = chunk * m
    kern = functools.partial(_s5_kernel, chunk=chunk, chunks_per_seq=chunks_per_seq)
    p4 = lambda i: (i, 0, 0, 0)
    p3 = lambda i: (i, 0, 0)
    return pl.pallas_call(
        kern,
        grid=(g,),
        in_specs=[
            pl.BlockSpec((chunk, m, r), lambda i: (0, i, 0)),
            pl.BlockSpec((1, 2, n, LANES), p4),
            pl.BlockSpec((1, SUBLANES, 2 * n), p3),
            pl.BlockSpec((1, 2, n, LANES), p4),
            pl.BlockSpec((1, m, 2 * n), p3),
            pl.BlockSpec((1, m, 2 * n), p3),
            pl.BlockSpec((1, m, 2 * n), p3),
            pl.BlockSpec((1, m, LANES), p3),
        ],
        out_specs=pl.BlockSpec((chunk, m, r), lambda i: (0, i, 0)),
        out_shape=jax.ShapeDtypeStruct((chunk, g * m, r), BF16),
        scratch_shapes=[pltpu.VMEM((lm, lm), BF16)],
        compiler_params=_cparams(("parallel",)),
        name="s5scan",
    )(ug, lamc, lamr, bb, cc, ca, cbm, dsk)


def _out1_kernel(y_ref, gate_ref, x_ref, wg_ref, wo_ref, g_ref, b_ref, o_ref, *, alpha):
    w = y_ref.shape[1]
    glu = jnp.dot(wg_ref[...], y_ref[0], preferred_element_type=F32)
    ga, gb = glu[:w], glu[w:]
    v = ga * (1.0 / (1.0 + jnp.exp(-gb))) * _silu(gate_ref[0].astype(F32))
    h = jnp.dot(wo_ref[...], v.astype(BF16), preferred_element_type=F32)
    s = alpha * x_ref[0] + h
    mu = jnp.mean(s, axis=0, keepdims=True)
    sc = s - mu
    var = jnp.mean(sc * sc, axis=0, keepdims=True)
    o_ref[0] = sc * lax.rsqrt(var + LN_EPS) * g_ref[...] + b_ref[...]


def _out1(y3, ug, xt, wgt, wot, g_col, b_col, *, alpha):
    l, w, r = y3.shape
    d = xt.shape[1]
    gate_blk = 1
    return pl.pallas_call(
        functools.partial(_out1_kernel, alpha=alpha),
        grid=(l,),
        in_specs=[
            pl.BlockSpec((1, w, r), lambda i: (i, 0, 0)),
            pl.BlockSpec((1, w, r), lambda i: (i, gate_blk, 0)),
            pl.BlockSpec((1, d, r), lambda i: (i, 0, 0)),
            _resident((2 * w, w), lambda i: (0, 0)),
            _resident((d, w), lambda i: (0, 0)),
            _resident((d, 1), lambda i: (0, 0)),
            _resident((d, 1), lambda i: (0, 0)),
        ],
        out_specs=pl.BlockSpec((1, d, r), lambda i: (i, 0, 0)),
        out_shape=jax.ShapeDtypeStruct((l, d, r), F32),
        compiler_params=_cparams(("parallel",)),
        name="out1",
    )(y3, ug, xt, wgt, wot, g_col, b_col)


def _pad_rows(v, rows=SUBLANES):
    v = jnp.atleast_2d(v.astype(F32))
    return jnp.pad(v, ((0, rows - v.shape[0]), (0, 0)))


def _pad_lanes(v, lanes=LANES):
    return jnp.pad(v, [(0, 0)] * (v.ndim - 1) + [(0, lanes - v.shape[-1])])


def _layer0(x2, batch, seq, in_w, conv_w, conv_b, dt_bias, a_log, d_skip, norm_g, out_w, ln_g, ln_b, alpha):
    d = x2.shape[1]
    ssd_heads = dt_bias.shape[0]
    width = ssd_heads * SSD_HEAD_DIM
    conv_dim = conv_w.shape[1]
    d_state = (conv_dim - width) // (2 * SSD_GROUPS)
    att_width = (in_w.shape[1] - width - conv_dim - ssd_heads) // 4
    att_heads = att_width // ATT_HEAD_DIM

    z_end = width
    xbc_end = z_end + conv_dim
    dt_end = xbc_end + ssd_heads
    w_main = jnp.concatenate([in_w[:, :xbc_end], in_w[:, dt_end:]], axis=1).astype(BF16)
    w_dt = _pad_lanes(in_w[:, xbc_end:dt_end]).astype(BF16)
    z_col, xbc_col = 0, z_end
    q_col = xbc_end
    k_col, v_col, g_col = q_col + att_width, q_col + 2 * att_width, q_col + 3 * att_width

    proj, dt_raw = _inproj(x2, w_main, w_dt, tm=1024, tn=1024)

    expand = (jnp.arange(LANES)[:, None] == (jnp.arange(width)[None, :] // SSD_HEAD_DIM)).astype(F32)
    ya = _ssd(proj, dt_raw, _pad_rows(conv_w), _pad_rows(conv_b), _pad_rows(_pad_lanes(dt_bias.astype(F32))),
              _pad_rows(_pad_lanes(-jnp.exp(a_log.astype(F32)))),
              _pad_rows(jnp.repeat(d_skip.astype(F32), SSD_HEAD_DIM)), _pad_rows(norm_g), expand,
              batch=batch, seq=seq, width=width, n_groups=SSD_GROUPS, d_state=d_state,
              z_col=z_col, xbc_col=xbc_col)

    half = ROPE_DIM // 2
    inv_freq = ROPE_THETA ** (-(jnp.arange(half, dtype=F32) * 2.0 / ROPE_DIM))
    ang = jnp.arange(seq, dtype=F32)[:, None] * inv_freq[None, :]
    ones = jnp.ones((seq, ATT_HEAD_DIM - ROPE_DIM), F32)
    cos_t = jnp.concatenate([jnp.cos(ang), jnp.cos(ang), ones], axis=1)
    sin_t = jnp.concatenate([-jnp.sin(ang), jnp.sin(ang), 0.0 * ones], axis=1)
    yb = _moba(proj, cos_t, sin_t, batch=batch, seq=seq, heads=att_heads,
               q_col=q_col, k_col=k_col, v_col=v_col, g_col=g_col, heads_per_step=4)

    return _out0(ya, yb, x2, out_w.astype(BF16), _pad_rows(ln_g), _pad_rows(ln_b), alpha=alpha, tm=512)


def _layer1(x2, batch, seq, in_w, lam_re, lam_im, log_dt, b_re, b_im, c_re, c_im, d_skip, glu_w, out_w,
            ln_g, ln_b, alpha):
    t, d = x2.shape
    chunk = S5_L
    cps = seq // chunk
    r = batch * cps
    groups, n = lam_re.shape
    m = S5_GROUP
    w = groups * m

    lre, lim = lam_re.astype(F32), lam_im.astype(F32)
    dt = jnp.exp(log_dt.astype(F32))[:, None]
    mag = jnp.exp(lre * dt)
    lbr, lbi = mag * jnp.cos(lim * dt), mag * jnp.sin(lim * dt)
    den = lre * lre + lim * lim
    fr = ((lbr - 1.0) * lre + lbi * lim) / den
    fi = (lbi * lre - (lbr - 1.0) * lim) / den
    bre, bim = b_re.astype(F32), b_im.astype(F32)
    bbr = fr[..., None] * bre - fi[..., None] * bim
    bbi = fr[..., None] * bim + fi[..., None] * bre
    lamc = jnp.broadcast_to(jnp.stack([lbr, lbi], axis=1)[..., None], (groups, 2, n, LANES))
    lamr = jnp.stack([jnp.concatenate([lbr, lbr], -1), jnp.concatenate([lbi, lbi], -1)], axis=1)
    lamr = jnp.pad(lamr, ((0, 0), (0, SUBLANES - 2), (0, 0)))
    bb = jnp.tile(jnp.stack([bbr, bbi], axis=1), (1, 1, 1, LANES // m))
    cr, ci = c_re.astype(F32), c_im.astype(F32)
    cc = jnp.concatenate([cr, -ci], axis=-1)
    cbm = jnp.concatenate([-ci, -cr], axis=-1)
    dsk = jnp.broadcast_to(d_skip.astype(F32).reshape(groups, m, 1), (groups, m, LANES))

    xt = x2.reshape(r, chunk, d).transpose(1, 2, 0)
    ug = _inproj1(xt, in_w.T.astype(BF16))
    y3 = _s5(ug, lamc, lamr, bb, cc, cc, cbm, dsk, chunks_per_seq=cps)
    o3 = _out1(y3, ug, xt, glu_w.T.astype(BF16), out_w.T.astype(BF16),
               ln_g.astype(F32)[:, None], ln_b.astype(F32)[:, None], alpha=alpha)
    return o3.transpose(2, 0, 1).reshape(t, d)


def kernel(x, in0_w, conv_w, conv_b, dt_bias, a_log, ssd_d, ssd_norm_g, out0_w, in1_w, s5_lam_re, s5_lam_im,
           s5_log_dt, s5_b_re, s5_b_im, s5_c_re, s5_c_im, s5_d, glu_w, out1_w, ln_g, ln_b):
    batch, seq, d = x.shape
    depth = ln_g.shape[0]
    alpha = (2 * depth) ** 0.25
    x2 = x.reshape(batch * seq, d)
    for layer in range(depth):
        i = layer // 2
        if layer % 2 == 0:
            x2 = _layer0(x2, batch, seq, in0_w[i], conv_w[i], conv_b[i], dt_bias[i], a_log[i], ssd_d[i],
                         ssd_norm_g[i], out0_w[i], ln_g[layer], ln_b[layer], alpha)
        else:
            x2 = _layer1(x2, batch, seq, in1_w[i], s5_lam_re[i], s5_lam_im[i], s5_log_dt[i], s5_b_re[i],
                         s5_b_im[i], s5_c_re[i], s5_c_im[i], s5_d[i], glu_w[i], out1_w[i],
                         ln_g[layer], ln_b[layer], alpha)
    return x2.reshape(batch, seq, d).astype(x.dtype)
```

```python
import functools
import math

import jax
import jax.numpy as jnp
from jax import lax
from jax.experimental import pallas as pl
from jax.experimental.pallas import tpu as pltpu

F32 = jnp.float32
BF16 = jnp.bfloat16

SSD_HEAD_DIM = 64
SSD_GROUPS = 4
SSD_STATE = 128
SSD_CONV = 4
SSD_CHUNK = 256
ATT_HEAD_DIM = 128
MOBA_BLOCK = 256
MOBA_TOPK = 3
ROPE_THETA = 500000.0
ROPE_DIM = ATT_HEAD_DIM // 4
S5_GROUP = 16
S5_STATE = 64
LN_EPS = 1e-5
RMS_EPS = 1e-5
NEG_INF = -1e30

LANES = 128
SUBLANES = 8
VMEM_LIMIT = 56 * 1024 * 1024

S5_L = 64


def _cparams(sem):
    return pltpu.CompilerParams(dimension_semantics=sem, vmem_limit_bytes=VMEM_LIMIT)


def _resident(shape, index_map):
    return pl.BlockSpec(shape, index_map, pipeline_mode=pl.Buffered(1))


def _sigmoid(v):
    return 0.5 + 0.5 * jnp.tanh(0.5 * v)


def _silu(v):
    return v * _sigmoid(v)


def _nt_dot(a, b):
    return lax.dot_general(a, b, (((1,), (1,)), ((), ())), preferred_element_type=F32)


def _tn_dot(a, b):
    return lax.dot_general(a, b, (((0,), (0,)), ((), ())), preferred_element_type=F32)


def _dot_hi(a, b):
    a_hi = a.astype(BF16)
    b_hi = b.astype(BF16)
    a_lo = (a - a_hi.astype(F32)).astype(BF16)
    b_lo = (b - b_hi.astype(F32)).astype(BF16)
    return (jnp.dot(a_hi, b_hi, preferred_element_type=F32) + jnp.dot(a_hi, b_lo, preferred_element_type=F32)
            + jnp.dot(a_lo, b_hi, preferred_element_type=F32))


def _dot_split(a, b, passes, split_lhs=True):
    rem = a if split_lhs else b
    acc = None
    for _ in range(passes):
        piece = rem.astype(BF16)
        part = jnp.dot(piece, b, preferred_element_type=F32) if split_lhs else jnp.dot(
            a, piece, preferred_element_type=F32)
        acc = part if acc is None else acc + part
        rem = rem - piece.astype(F32)
    return acc


def _inproj_kernel(x_ref, wa_ref, wb_ref, wdt_ref, o_ref, dt_ref, xb_ref, *, tiles_a):
    j = pl.program_id(1)

    @pl.when(j == 0)
    def _():
        xb = x_ref[...].astype(BF16)
        xb_ref[...] = xb
        dt_ref[...] = jnp.dot(xb, wdt_ref[...], preferred_element_type=F32)

    @pl.when(j < tiles_a)
    def _():
        o_ref[...] = jnp.dot(xb_ref[...], wa_ref[...], preferred_element_type=F32).astype(o_ref.dtype)

    @pl.when(j >= tiles_a)
    def _():
        o_ref[...] = jnp.dot(xb_ref[...], wb_ref[...], preferred_element_type=F32).astype(o_ref.dtype)


def _inproj(x2, w_a, w_b, w_dt, tm, tn):
    t, d = x2.shape
    tiles_a, tiles_b = w_a.shape[1] // tn, w_b.shape[1] // tn
    n = (tiles_a + tiles_b) * tn
    return pl.pallas_call(
        functools.partial(_inproj_kernel, tiles_a=tiles_a),
        grid=(t // tm, tiles_a + tiles_b),
        in_specs=[
            pl.BlockSpec((tm, d), lambda i, j: (i, 0)),
            pl.BlockSpec((d, tn), lambda i, j: (0, jnp.minimum(j, tiles_a - 1))),
            pl.BlockSpec((d, tn), lambda i, j: (0, jnp.maximum(j - tiles_a, 0))),
            _resident((d, LANES), lambda i, j: (0, 0)),
        ],
        out_specs=[
            pl.BlockSpec((tm, tn), lambda i, j: (i, j)),
            pl.BlockSpec((tm, LANES), lambda i, j: (i, 0)),
        ],
        out_shape=[jax.ShapeDtypeStruct((t, n), BF16), jax.ShapeDtypeStruct((t, LANES), F32)],
        scratch_shapes=[pltpu.VMEM((tm, d), BF16)],
        compiler_params=_cparams(("parallel", "arbitrary")),
        name="inproj0",
    )(x2, w_a, w_b, w_dt)


def _ssd_kernel(z_ref, xs_ref, bm_ref, cm_ref, dt_ref, cw_ref, cb_ref, dtb_ref, aneg_ref, dsk_ref, ng_ref,
                exp_ref, o_ref, xpad_ref, state_ref, act_ref, y_ref, *, n_groups, head_dim):
    c = pl.program_id(1)
    chunk, width = z_ref.shape
    d_state = bm_ref.shape[1] // n_groups
    heads_per_group = width // head_dim // n_groups
    gw = heads_per_group * head_dim

    @pl.when(c == 0)
    def _():
        xpad_ref[0:SUBLANES, :] = jnp.zeros((SUBLANES, xpad_ref.shape[1]), F32)
        state_ref[...] = jnp.zeros_like(state_ref)

    strip = 512
    col0 = 0
    for src in (xs_ref, bm_ref, cm_ref):
        for s0 in range(0, src.shape[1], strip):
            cols = slice(col0 + s0, col0 + s0 + strip)
            xs_ = src[:, s0:s0 + strip].astype(F32)
            xpad_ref[SUBLANES:, cols] = xs_
            acc = cb_ref[0:1, cols] + cw_ref[SSD_CONV - 1:SSD_CONV, cols] * xs_
            for k in range(1, SSD_CONV):
                acc = acc + cw_ref[SSD_CONV - 1 - k:SSD_CONV - k, cols] * xpad_ref[SUBLANES - k:SUBLANES - k + chunk, cols]
            act_ref[:, cols] = _silu(acc)
            xpad_ref[0:SUBLANES, cols] = xs_[chunk - SUBLANES:, :]
        col0 += src.shape[1]

    dt_in = dt_ref[...] + dtb_ref[0:1, :]
    dt = jnp.maximum(dt_in, 0.0) + jnp.log(1.0 + jnp.exp(-jnp.abs(dt_in)))
    a = dt * aneg_ref[0:1, :]
    row = lax.broadcasted_iota(jnp.int32, (chunk, chunk), 0)
    col = lax.broadcasted_iota(jnp.int32, (chunk, chunk), 1)
    causal = row >= col
    cs = _dot_split(jnp.where(causal, 1.0, 0.0).astype(BF16), a, 3, split_lhs=False)
    cs2 = cs * math.log2(math.e)
    cs2_t = cs2.T
    cs_last = cs[chunk - 1:chunk, :]
    expand = exp_ref[...]
    dt_x = _dot_split(dt, expand, 2)
    ecs_x = _dot_split(jnp.exp(cs), expand, 2)
    dec_x = _dot_split(jnp.exp(cs_last - cs), expand, 2)
    sdec_x = _dot_split(jnp.broadcast_to(jnp.exp(cs_last), (SUBLANES, LANES)), expand, 2)[0:1, :]

    lane = lax.broadcasted_iota(jnp.int32, (chunk, LANES), 1)
    first_half = lane < head_dim
    for g in range(n_groups):
        gs = slice(g * gw, (g + 1) * gw)
        b_g = act_ref[:, width + g * d_state:width + (g + 1) * d_state].astype(BF16)
        c_g = act_ref[:, width + (n_groups + g) * d_state:width + (n_groups + g + 1) * d_state].astype(BF16)
        xs_g = act_ref[:, gs]
        xdt = xs_g * dt_x[:, gs]
        xdt_b = xdt.astype(BF16)
        cb = jnp.where(causal, _nt_dot(c_g, b_g), 0.0)
        st = state_ref[:, gs]
        y_g = jnp.dot(c_g, st.astype(BF16), preferred_element_type=F32) * ecs_x[:, gs] + dsk_ref[0:1, gs] * xs_g
        state_ref[:, gs] = st * sdec_x[:, gs] + _tn_dot(b_g, (xdt * dec_x[:, gs]).astype(BF16))
        for pr in range(heads_per_group // 2):
            xp = xdt_b[:, pr * LANES:(pr + 1) * LANES]
            halves = []
            for half in range(2):
                h = g * heads_per_group + 2 * pr + half
                seg = cs2[:, h:h + 1] - cs2_t[h:h + 1, :]
                m = (cb * jnp.exp2(jnp.minimum(seg, 0.0))).astype(BF16)
                halves.append(jnp.dot(m, xp, preferred_element_type=F32))
            y_pair = jnp.where(first_half, halves[0], halves[1])
            y_ref[:, g * gw + pr * LANES:g * gw + (pr + 1) * LANES] = y_pair + y_g[:, pr * LANES:(pr + 1) * LANES]

    yz = y_ref[...] * _silu(z_ref[...].astype(F32))
    ms = jnp.mean(yz * yz, axis=1, keepdims=True)
    o_ref[...] = (yz * lax.rsqrt(ms + RMS_EPS) * ng_ref[0:1, :]).astype(o_ref.dtype)


def _ssd(proj, dt_raw, cw, cb, dtb, aneg, dsk, ng, expand, *, batch, seq, width, n_groups, d_state,
         z_col, xbc_col):
    conv_dim = width + 2 * n_groups * d_state
    bc_w = n_groups * d_state
    nc = seq // SSD_CHUNK
    t = batch * seq
    kern = functools.partial(_ssd_kernel, n_groups=n_groups, head_dim=SSD_HEAD_DIM)
    b_col, c_col = xbc_col + width, xbc_col + width + bc_w
    assert z_col % width == 0 and xbc_col % width == 0 and b_col % bc_w == 0 and c_col % bc_w == 0
    const = lambda b, c: (0, 0)
    return pl.pallas_call(
        kern,
        grid=(batch, nc),
        in_specs=[
            pl.BlockSpec((SSD_CHUNK, width), lambda b, c: (b * nc + c, z_col // width)),
            pl.BlockSpec((SSD_CHUNK, width), lambda b, c: (b * nc + c, xbc_col // width)),
            pl.BlockSpec((SSD_CHUNK, bc_w), lambda b, c: (b * nc + c, b_col // bc_w)),
            pl.BlockSpec((SSD_CHUNK, bc_w), lambda b, c: (b * nc + c, c_col // bc_w)),
            pl.BlockSpec((SSD_CHUNK, LANES), lambda b, c: (b * nc + c, 0)),
            _resident((SUBLANES, conv_dim), const),
            _resident((SUBLANES, conv_dim), const),
            _resident((SUBLANES, LANES), const),
            _resident((SUBLANES, LANES), const),
            _resident((SUBLANES, width), const),
            _resident((SUBLANES, width), const),
            _resident((LANES, width), const),
        ],
        out_specs=pl.BlockSpec((SSD_CHUNK, width), lambda b, c: (b * nc + c, 0)),
        out_shape=jax.ShapeDtypeStruct((t, width), BF16),
        scratch_shapes=[
            pltpu.VMEM((SUBLANES + SSD_CHUNK, conv_dim), F32),
            pltpu.VMEM((d_state, width), F32),
            pltpu.VMEM((SSD_CHUNK, conv_dim), F32),
            pltpu.VMEM((SSD_CHUNK, width), F32),
        ],
        compiler_params=_cparams(("parallel", "arbitrary")),
        name="ssd0",
    )(proj, proj, proj, proj, dt_raw, cw, cb, dtb, aneg, dsk, ng, expand)


def _rope(v, cos, sin):
    half = ROPE_DIM // 2
    lane = lax.broadcasted_iota(jnp.int32, v.shape, 1)
    partner = jnp.where(lane < half, pltpu.roll(v, LANES - half, 1), pltpu.roll(v, half, 1))
    return v * cos + partner * sin


def _moba_kernel(q_ref, k_ref, v_ref, g_ref, cos_ref, sin_ref, o_ref, kr_ref, vt_ref, km_ref, ch_ref,
                 sa_ref, sb_ref, *, n_blocks, heads_per_step):
    qb = pl.program_id(2)
    blk = MOBA_BLOCK
    hd = ATT_HEAD_DIM
    scale = hd ** -0.5 * math.log2(math.e)
    heads = range(heads_per_step)

    @pl.when(qb == 0)
    def _():
        for n in range(n_blocks):
            rows = slice(n * blk, (n + 1) * blk)
            for j in heads:
                cols = slice(j * hd, (j + 1) * hd)
                kr = _rope(k_ref[rows, cols].astype(F32), cos_ref[rows, :], sin_ref[rows, :])
                kr_ref[j, rows, :] = kr.astype(BF16)
                km_ref[j, n:n + 1, :] = jnp.mean(kr, axis=0, keepdims=True)
                vt_ref[j, 0:hd, rows] = v_ref[rows, cols].astype(F32).T.astype(BF16)
                vt_ref[j, hd:, rows] = jnp.ones((vt_ref.shape[1] - hd, blk), BF16)

    q0 = pl.multiple_of(qb * blk, blk)
    cos_q = cos_ref[pl.ds(q0, blk), :]
    sin_q = sin_ref[pl.ds(q0, blk), :]
    blk_id = lax.broadcasted_iota(jnp.int32, (n_blocks, blk), 0)
    key_i = lax.broadcasted_iota(jnp.int32, (blk, blk), 0)
    qry_i = lax.broadcasted_iota(jnp.int32, (blk, blk), 1)

    qs_t, init = [], []
    for j in heads:
        qr_t = _rope(q_ref[:, j * hd:(j + 1) * hd].astype(F32), cos_q, sin_q).T
        qs_t.append((qr_t * scale).astype(BF16))

        gate = jnp.where(blk_id < qb, _dot_hi(km_ref[j], qr_t), NEG_INF)
        rank = jnp.zeros((n_blocks, blk), jnp.int32)
        for n2 in range(n_blocks - 1):
            other = gate[n2:n2 + 1, :]
            rank = rank + jnp.where(other > gate, 1,
                                    jnp.where(other == gate, jnp.where(blk_id > n2, 1, 0), 0))
        ch_ref[j] = jnp.where(blk_id < qb, jnp.where(rank < MOBA_TOPK, 1.0, 0.0), 0.0)

        s = jnp.dot(kr_ref[j, pl.ds(q0, blk), :], qs_t[j], preferred_element_type=F32)
        s = jnp.where(key_i <= qry_i, s, NEG_INF)
        m0 = jnp.max(s, axis=0, keepdims=True)
        p = jnp.exp2(s - m0).astype(BF16)
        init.append((m0, jnp.dot(vt_ref[j, :, pl.ds(q0, blk)], p, preferred_element_type=F32)))

    def stage(n, ref):
        n0 = pl.multiple_of(jnp.minimum(n, n_blocks - 1) * blk, blk)
        for j in heads:
            ref[j] = jnp.dot(kr_ref[j, pl.ds(n0, blk), :], qs_t[j], preferred_element_type=F32)

    def consume(n, ref, carry):
        n0 = pl.multiple_of(n * blk, blk)
        out = []
        for j in heads:
            m, acc = carry[j]
            taken = ch_ref[j, pl.ds(n, 1), :] > 0.5
            m_new = jnp.maximum(m, jnp.where(taken, jnp.max(ref[j], axis=0, keepdims=True), NEG_INF))
            alpha = jnp.exp2(m - m_new)
            p = jnp.exp2(ref[j] - jnp.where(taken, m_new, -NEG_INF)).astype(BF16)
            out.append((m_new, alpha * acc + jnp.dot(vt_ref[j, :, pl.ds(n0, blk)], p,
                                                     preferred_element_type=F32)))
        return tuple(out)

    def two_blocks(i, carry):
        stage(2 * i + 1, sb_ref)
        carry = consume(2 * i, sa_ref, carry)
        stage(2 * i + 2, sa_ref)
        return consume(2 * i + 1, sb_ref, carry)

    stage(0, sa_ref)
    final = lax.fori_loop(0, (qb + 1) // 2, two_blocks, tuple(init))
    for j in heads:
        _, acc = final[j]
        cols = slice(j * hd, (j + 1) * hd)
        out_t = acc[0:hd, :] / acc[hd:hd + 1, :]
        o_ref[:, cols] = (out_t.T * _silu(g_ref[:, cols].astype(F32))).astype(o_ref.dtype)


def _moba(proj, cos_t, sin_t, *, batch, seq, heads, q_col, k_col, v_col, g_col, heads_per_step):
    nb = seq // MOBA_BLOCK
    hd = ATT_HEAD_DIM
    hw = hd * heads_per_step
    t = batch * seq
    assert heads % heads_per_step == 0 and all(c % hw == 0 for c in (q_col, k_col, v_col, g_col))
    qc, kc, vc, gc = (c // hw for c in (q_col, k_col, v_col, g_col))
    kern = functools.partial(_moba_kernel, n_blocks=nb, heads_per_step=heads_per_step)
    return pl.pallas_call(
        kern,
        grid=(batch, heads // heads_per_step, nb),
        in_specs=[
            pl.BlockSpec((MOBA_BLOCK, hw), lambda b, h, i: (b * nb + i, qc + h)),
            pl.BlockSpec((seq, hw), lambda b, h, i: (b, kc + h)),
            pl.BlockSpec((seq, hw), lambda b, h, i: (b, vc + h)),
            pl.BlockSpec((MOBA_BLOCK, hw), lambda b, h, i: (b * nb + i, gc + h)),
            _resident((seq, hd), lambda b, h, i: (0, 0)),
            _resident((seq, hd), lambda b, h, i: (0, 0)),
        ],
        out_specs=pl.BlockSpec((MOBA_BLOCK, hw), lambda b, h, i: (b * nb + i, h)),
        out_shape=jax.ShapeDtypeStruct((t, heads * hd), BF16),
        scratch_shapes=[
            pltpu.VMEM((heads_per_step, seq, hd), BF16),
            pltpu.VMEM((heads_per_step, hd + 16, seq), BF16),
            pltpu.VMEM((heads_per_step, nb, hd), F32),
            pltpu.VMEM((heads_per_step, nb, MOBA_BLOCK), F32),
            pltpu.VMEM((heads_per_step, MOBA_BLOCK, MOBA_BLOCK), F32),
            pltpu.VMEM((heads_per_step, MOBA_BLOCK, MOBA_BLOCK), F32),
        ],
        compiler_params=_cparams(("parallel", "parallel", "arbitrary")),
        name="moba0",
    )(proj, proj, proj, proj, cos_t, sin_t)


def _out0_kernel(ya_ref, yb_ref, x_ref, w_ref, g_ref, b_ref, o_ref, *, alpha):
    ka = ya_ref.shape[1]
    h = jnp.dot(ya_ref[...], w_ref[0:ka, :], preferred_element_type=F32)
    h = h + jnp.dot(yb_ref[...], w_ref[ka:, :], preferred_element_type=F32)
    v = alpha * x_ref[...] + h
    mu = jnp.mean(v, axis=1, keepdims=True)
    vc = v - mu
    var = jnp.mean(vc * vc, axis=1, keepdims=True)
    o_ref[...] = vc * lax.rsqrt(var + LN_EPS) * g_ref[0:1, :] + b_ref[0:1, :]


def _out0(ya, yb, x2, w, g, b, *, alpha, tm):
    t, d = x2.shape
    ka, kb = ya.shape[1], yb.shape[1]
    return pl.pallas_call(
        functools.partial(_out0_kernel, alpha=alpha),
        grid=(t // tm,),
        in_specs=[
            pl.BlockSpec((tm, ka), lambda i: (i, 0)),
            pl.BlockSpec((tm, kb), lambda i: (i, 0)),
            pl.BlockSpec((tm, d), lambda i: (i, 0)),
            _resident((ka + kb, d), lambda i: (0, 0)),
            _resident((SUBLANES, d), lambda i: (0, 0)),
            _resident((SUBLANES, d), lambda i: (0, 0)),
        ],
        out_specs=pl.BlockSpec((tm, d), lambda i: (i, 0)),
        out_shape=jax.ShapeDtypeStruct((t, d), F32),
        compiler_params=_cparams(("parallel",)),
        name="out0",
    )(ya, yb, x2, w, g, b)


def _inproj1_kernel(x_ref, w_ref, o_ref):
    o_ref[0] = jnp.dot(w_ref[...], x_ref[0].astype(BF16), preferred_element_type=F32).astype(o_ref.dtype)


def _inproj1(xt, wt):
    l, d, r = xt.shape
    n = wt.shape[0]
    return pl.pallas_call(
        _inproj1_kernel,
        grid=(l,),
        in_specs=[pl.BlockSpec((1, d, r), lambda i: (i, 0, 0)), _resident((n, d), lambda i: (0, 0))],
        out_specs=pl.BlockSpec((1, n, r), lambda i: (i, 0, 0)),
        out_shape=jax.ShapeDtypeStruct((l, n, r), BF16),
        compiler_params=_cparams(("parallel",)),
        name="inproj1",
    )(xt, wt)


def _cpow(lr, li, d, nbits):
    pr = jnp.ones(d.shape, F32)
    pi = jnp.zeros(d.shape, F32)
    br, bi = lr, li
    for bit in range(nbits):
        on = ((d >> bit) & 1) == 1
        nr = pr * br - pi * bi
        ni = pr * bi + pi * br
        pr = jnp.where(on, nr, pr)
        pi = jnp.where(on, ni, pi)
        if bit + 1 < nbits:
            br, bi = br * br - bi * bi, 2.0 * br * bi
    return pr, pi


def _gelu_tanh(v):
    return 0.5 * v * (1.0 + jnp.tanh(math.sqrt(2.0 / math.pi) * (v + 0.044715 * (v * v * v))))


def _s5_kernel(u_ref, lamc_ref, lamr_ref, bb_ref, rep_ref, cc_ref, ca_ref, cbm_ref, dsk_ref, o_ref, tz_ref, cl_ref,
               *, chunk, chunks_per_seq):
    m = S5_GROUP
    n = S5_STATE
    lm = chunk * m
    r = u_ref.shape[2]
    u = u_ref[...].reshape(lm, r)

    pos_per_blk = LANES // m
    n_lane_blk = lm // LANES
    lam_r, lam_i = lamc_ref[0, 0], lamc_ref[0, 1]
    lane = lax.broadcasted_iota(jnp.int32, (n, LANES), 1)
    pr, pi = _cpow(lam_r, lam_i, (pos_per_blk - 1) - lane // m, (pos_per_blk - 1).bit_length())
    lbr, lbi = lam_r, lam_i
    for _ in range(pos_per_blk.bit_length() - 1):
        lbr, lbi = lbr * lbr - lbi * lbi, 2.0 * lbr * lbi
    bbr = _dot_split(bb_ref[0, 0], rep_ref[...], 3)
    bbi = _dot_split(bb_ref[0, 1], rep_ref[...], 3)
    bl_r, bl_i = [None] * n_lane_blk, [None] * n_lane_blk
    for c in range(n_lane_blk - 1, -1, -1):
        bl_r[c] = pr * bbr - pi * bbi
        bl_i[c] = pr * bbi + pi * bbr
        pr, pi = pr * lbr - pi * lbi, pr * lbi + pi * lbr
    bl = jnp.concatenate([jnp.concatenate(bl_r, axis=1), jnp.concatenate(bl_i, axis=1)], axis=0)

    krev = _dot_hi(cc_ref[0], bl)

    col_s = lax.broadcasted_iota(jnp.int32, (m, lm), 1) // m
    for t in range(chunk):
        rolled = pltpu.roll(krev, ((t + 1) * m) % lm, 1)
        tz_ref[t * m:(t + 1) * m, :] = jnp.where(col_s <= t, rolled, 0.0).astype(BF16)

    y = jnp.dot(tz_ref[...], u, preferred_element_type=F32)

    e = jnp.dot(bl.astype(BF16), u, preferred_element_type=F32)
    lane_r = lax.broadcasted_iota(jnp.int32, (n, r), 1) % chunks_per_seq
    hr = jnp.where(lane_r >= 1, pltpu.roll(e[:n], 1, 1), 0.0)
    hi = jnp.where(lane_r >= 1, pltpu.roll(e[n:], 1, 1), 0.0)
    mr, mi = lbr, lbi
    for _ in range((chunk // pos_per_blk).bit_length() - 1):
        mr, mi = mr * mr - mi * mi, 2.0 * mr * mi
    mr = jnp.concatenate([mr] * (r // LANES), axis=1) if r > LANES else mr[:, :r]
    mi = jnp.concatenate([mi] * (r // LANES), axis=1) if r > LANES else mi[:, :r]
    step = 1
    while step < chunks_per_seq:
        sr = jnp.where(lane_r >= step, pltpu.roll(hr, step, 1), 0.0)
        si = jnp.where(lane_r >= step, pltpu.roll(hi, step, 1), 0.0)
        hr, hi = hr + mr * sr - mi * si, hi + mr * si + mi * sr
        mr, mi = mr * mr - mi * mi, 2.0 * mr * mi
        step *= 2
    h_in = jnp.concatenate([hr, hi], axis=0).astype(BF16)

    lam_rr = jnp.broadcast_to(lamr_ref[0, 0:1, :], (m, 2 * n))
    lam_ri = jnp.broadcast_to(lamr_ref[0, 1:2, :], (m, 2 * n))
    ca, cbm = ca_ref[0], cbm_ref[0]
    qr, qi = lam_rr, lam_ri
    for t in range(chunk):
        cl_ref[t * m:(t + 1) * m, :] = (ca * qr + cbm * qi).astype(BF16)
        if t + 1 < chunk:
            qr, qi = qr * lam_rr - qi * lam_ri, qr * lam_ri + qi * lam_rr
    y = y + jnp.dot(cl_ref[...], h_in, preferred_element_type=F32)

    dsk = jnp.concatenate([dsk_ref[0]] * chunk, axis=0)
    dsk = jnp.concatenate([dsk] * (r // LANES), axis=1) if r > LANES else dsk[:, :r]
    y = y + dsk * u.astype(F32)
    o_ref[...] = _gelu_tanh(y).astype(o_ref.dtype).reshape(chunk, m, r)


def _s5(ug, lamc, lamr, bb, rep, cc, ca, cbm, dsk, *, chunks_per_seq):
    chunk, _, r = ug.shape
    g = lamc.shape[0]
    m, n = S5_GROUP, S5_STATE
    lm = chunk * m
    kern = functools.partial(_s5_kernel, chunk=chunk, chunks_per_seq=chunks_per_seq)
    p4 = lambda i: (i, 0, 0, 0)
    p3 = lambda i: (i, 0, 0)
    return pl.pallas_call(
        kern,
        grid=(g,),
        in_specs=[
            pl.BlockSpec((chunk, m, r), lambda i: (0, i, 0)),
            pl.BlockSpec((1, 2, n, LANES), p4),
            pl.BlockSpec((1, SUBLANES, 2 * n), p3),
            pl.BlockSpec((1, 2, n, m), p4),
            _resident((m, LANES), lambda i: (0, 0)),
            pl.BlockSpec((1, m, 2 * n), p3),
            pl.BlockSpec((1, m, 2 * n), p3),
            pl.BlockSpec((1, m, 2 * n), p3),
            pl.BlockSpec((1, m, LANES), p3),
        ],
        out_specs=pl.BlockSpec((chunk, m, r), lambda i: (0, i, 0)),
        out_shape=jax.ShapeDtypeStruct((chunk, g * m, r), BF16),
        scratch_shapes=[pltpu.VMEM((lm, lm), BF16), pltpu.VMEM((lm, 2 * n), BF16)],
        compiler_params=_cparams(("parallel",)),
        name="s5scan",
    )(ug, lamc, lamr, bb, rep, cc, ca, cbm, dsk)


def _out1_kernel(y_ref, gate_ref, x_ref, wg_ref, wo_ref, g_ref, b_ref, o_ref, v_ref, s_ref, *, alpha):
    w = y_ref.shape[1]
    d = x_ref.shape[1]
    mc = 512
    y = y_ref[0]
    for c0 in range(0, w, mc):
        ga = jnp.dot(wg_ref[c0:c0 + mc, :], y, preferred_element_type=F32)
        gb = jnp.dot(wg_ref[w + c0:w + c0 + mc, :], y, preferred_element_type=F32)
        v_ref[c0:c0 + mc, :] = (ga * _sigmoid(gb) * _silu(gate_ref[0, c0:c0 + mc, :].astype(F32))).astype(BF16)
    v = v_ref[...]
    for c0 in range(0, d, mc):
        s_ref[c0:c0 + mc, :] = alpha * x_ref[0, c0:c0 + mc, :] + jnp.dot(
            wo_ref[c0:c0 + mc, :], v, preferred_element_type=F32)
    s = s_ref[...]
    mu = jnp.mean(s, axis=0, keepdims=True)
    sc = s - mu
    var = jnp.mean(sc * sc, axis=0, keepdims=True)
    o_ref[0] = sc * lax.rsqrt(var + LN_EPS) * g_ref[...] + b_ref[...]


def _out1(y3, ug, xt, wgt, wot, g_col, b_col, *, alpha):
    l, w, r = y3.shape
    d = xt.shape[1]
    gate_blk = 1
    return pl.pallas_call(
        functools.partial(_out1_kernel, alpha=alpha),
        grid=(l,),
        in_specs=[
            pl.BlockSpec((1, w, r), lambda i: (i, 0, 0)),
            pl.BlockSpec((1, w, r), lambda i: (i, gate_blk, 0)),
            pl.BlockSpec((1, d, r), lambda i: (i, 0, 0)),
            _resident((2 * w, w), lambda i: (0, 0)),
            _resident((d, w), lambda i: (0, 0)),
            _resident((d, 1), lambda i: (0, 0)),
            _resident((d, 1), lambda i: (0, 0)),
        ],
        out_specs=pl.BlockSpec((1, d, r), lambda i: (i, 0, 0)),
        out_shape=jax.ShapeDtypeStruct((l, d, r), F32),
        scratch_shapes=[pltpu.VMEM((w, r), BF16), pltpu.VMEM((d, r), F32)],
        compiler_params=_cparams(("parallel",)),
        name="out1",
    )(y3, ug, xt, wgt, wot, g_col, b_col)


def _pad_rows(v, rows=SUBLANES):
    v = jnp.atleast_2d(v.astype(F32))
    return jnp.pad(v, ((0, rows - v.shape[0]), (0, 0)))


def _pad_lanes(v, lanes=LANES):
    return jnp.pad(v, [(0, 0)] * (v.ndim - 1) + [(0, lanes - v.shape[-1])])


def _layer0(x2, batch, seq, in_w, conv_w, conv_b, dt_bias, a_log, d_skip, norm_g, out_w, ln_g, ln_b, alpha):
    d = x2.shape[1]
    ssd_heads = dt_bias.shape[0]
    width = ssd_heads * SSD_HEAD_DIM
    conv_dim = conv_w.shape[1]
    d_state = (conv_dim - width) // (2 * SSD_GROUPS)
    att_width = (in_w.shape[1] - width - conv_dim - ssd_heads) // 4
    att_heads = att_width // ATT_HEAD_DIM

    z_end = width
    xbc_end = z_end + conv_dim
    dt_end = xbc_end + ssd_heads
    w_a = in_w[:, :xbc_end].astype(BF16)
    w_b = in_w[:, dt_end:].astype(BF16)
    w_dt = _pad_lanes(in_w[:, xbc_end:dt_end]).astype(BF16)
    z_col, xbc_col = 0, z_end
    q_col = xbc_end
    k_col, v_col, g_col = q_col + att_width, q_col + 2 * att_width, q_col + 3 * att_width

    proj, dt_raw = _inproj(x2, w_a, w_b, w_dt, tm=1024, tn=1024)

    expand = (jnp.arange(LANES)[:, None] == (jnp.arange(width)[None, :] // SSD_HEAD_DIM)).astype(BF16)
    ya = _ssd(proj, dt_raw, _pad_rows(conv_w), _pad_rows(conv_b), _pad_rows(_pad_lanes(dt_bias.astype(F32))),
              _pad_rows(_pad_lanes(-jnp.exp(a_log.astype(F32)))),
              _pad_rows(jnp.repeat(d_skip.astype(F32), SSD_HEAD_DIM)), _pad_rows(norm_g), expand,
              batch=batch, seq=seq, width=width, n_groups=SSD_GROUPS, d_state=d_state,
              z_col=z_col, xbc_col=xbc_col)

    half = ROPE_DIM // 2
    inv_freq = ROPE_THETA ** (-(jnp.arange(half, dtype=F32) * 2.0 / ROPE_DIM))
    ang = jnp.arange(seq, dtype=F32)[:, None] * inv_freq[None, :]
    ones = jnp.ones((seq, ATT_HEAD_DIM - ROPE_DIM), F32)
    cos_t = jnp.concatenate([jnp.cos(ang), jnp.cos(ang), ones], axis=1)
    sin_t = jnp.concatenate([-jnp.sin(ang), jnp.sin(ang), 0.0 * ones], axis=1)
    yb = _moba(proj, cos_t, sin_t, batch=batch, seq=seq, heads=att_heads,
               q_col=q_col, k_col=k_col, v_col=v_col, g_col=g_col, heads_per_step=4)

    return _out0(ya, yb, x2, out_w.astype(BF16), _pad_rows(ln_g), _pad_rows(ln_b), alpha=alpha, tm=512)


def _layer1(x2, batch, seq, in_w, lam_re, lam_im, log_dt, b_re, b_im, c_re, c_im, d_skip, glu_w, out_w,
            ln_g, ln_b, alpha):
    t, d = x2.shape
    chunk = S5_L
    cps = seq // chunk
    r = batch * cps
    groups, n = lam_re.shape
    m = S5_GROUP
    w = groups * m

    lre, lim = lam_re.astype(F32), lam_im.astype(F32)
    dt = jnp.exp(log_dt.astype(F32))[:, None]
    mag = jnp.exp(lre * dt)
    lbr, lbi = mag * jnp.cos(lim * dt), mag * jnp.sin(lim * dt)
    den = lre * lre + lim * lim
    fr = ((lbr - 1.0) * lre + lbi * lim) / den
    fi = (lbi * lre - (lbr - 1.0) * lim) / den
    bre, bim = b_re.astype(F32), b_im.astype(F32)
    bbr = fr[..., None] * bre - fi[..., None] * bim
    bbi = fr[..., None] * bim + fi[..., None] * bre
    lamc = jnp.broadcast_to(jnp.stack([lbr, lbi], axis=1)[..., None], (groups, 2, n, LANES))
    lamr = jnp.stack([jnp.concatenate([lbr, lbr], -1), jnp.concatenate([lbi, lbi], -1)], axis=1)
    lamr = jnp.pad(lamr, ((0, 0), (0, SUBLANES - 2), (0, 0)))
    bb = jnp.stack([bbr, bbi], axis=1)
    rep = (jnp.arange(m)[:, None] == (jnp.arange(LANES)[None, :] % m)).astype(BF16)
    cr, ci = c_re.astype(F32), c_im.astype(F32)
    cc = jnp.concatenate([cr, -ci], axis=-1)
    cbm = jnp.concatenate([-ci, -cr], axis=-1)
    dsk = jnp.broadcast_to(d_skip.astype(F32).reshape(groups, m, 1), (groups, m, LANES))

    xt = x2.reshape(r, chunk, d).transpose(1, 2, 0)
    ug = _inproj1(xt, in_w.T.astype(BF16))
    y3 = _s5(ug, lamc, lamr, bb, rep, cc, cc, cbm, dsk, chunks_per_seq=cps)
    o3 = _out1(y3, ug, xt, glu_w.T.astype(BF16), out_w.T.astype(BF16),
               ln_g.astype(F32)[:, None], ln_b.astype(F32)[:, None], alpha=alpha)
    return o3.transpose(2, 0, 1).reshape(t, d)


def kernel(x, in0_w, conv_w, conv_b, dt_bias, a_log, ssd_d, ssd_norm_g, out0_w, in1_w, s5_lam_re, s5_lam_im,
           s5_log_dt, s5_b_re, s5_b_im, s5_c_re, s5_c_im, s5_d, glu_w, out1_w, ln_g, ln_b):
    batch, seq, d = x.shape
    depth = ln_g.shape[0]
    alpha = (2 * depth) ** 0.25
    x2 = x.reshape(batch * seq, d)
    for layer in range(depth):
        i = layer // 2
        if layer % 2 == 0:
            x2 = _layer0(x2, batch, seq, in0_w[i], conv_w[i], conv_b[i], dt_bias[i], a_log[i], ssd_d[i],
                         ssd_norm_g[i], out0_w[i], ln_g[layer], ln_b[layer], alpha)
        else:
            x2 = _layer1(x2, batch, seq, in1_w[i], s5_lam_re[i], s5_lam_im[i], s5_log_dt[i], s5_b_re[i],
                         s5_b_im[i], s5_c_re[i], s5_c_im[i], s5_d[i], glu_w[i], out1_w[i],
                         ln_g[layer], ln_b[layer], alpha)
    return x2.reshape(batch, seq, d).astype(x.dtype)
```

```python
import functools
import math

import jax
import jax.numpy as jnp
from jax import lax
from jax.experimental import pallas as pl
from jax.experimental.pallas import tpu as pltpu

F32 = jnp.float32
BF16 = jnp.bfloat16

SSD_HEAD_DIM = 64
SSD_GROUPS = 4
SSD_STATE = 128
SSD_CONV = 4
SSD_CHUNK = 256
ATT_HEAD_DIM = 128
MOBA_BLOCK = 256
MOBA_TOPK = 3
ROPE_THETA = 500000.0
ROPE_DIM = ATT_HEAD_DIM // 4
S5_GROUP = 16
S5_STATE = 64
LN_EPS = 1e-5
RMS_EPS = 1e-5
NEG_INF = -1e30

LANES = 128
SUBLANES = 8
VMEM_LIMIT = 56 * 1024 * 1024

S5_L = 64


def _cparams(sem):
    return pltpu.CompilerParams(dimension_semantics=sem, vmem_limit_bytes=VMEM_LIMIT)


def _resident(shape, index_map):
    return pl.BlockSpec(shape, index_map, pipeline_mode=pl.Buffered(1))


def _sigmoid(v):
    return 0.5 + 0.5 * jnp.tanh(0.5 * v)


def _silu(v):
    return v * _sigmoid(v)


def _nt_dot(a, b):
    return lax.dot_general(a, b, (((1,), (1,)), ((), ())), preferred_element_type=F32)


def _tn_dot(a, b):
    return lax.dot_general(a, b, (((0,), (0,)), ((), ())), preferred_element_type=F32)


def _dot_hi(a, b):
    a_hi = a.astype(BF16)
    b_hi = b.astype(BF16)
    a_lo = (a - a_hi.astype(F32)).astype(BF16)
    b_lo = (b - b_hi.astype(F32)).astype(BF16)
    return (jnp.dot(a_hi, b_hi, preferred_element_type=F32) + jnp.dot(a_hi, b_lo, preferred_element_type=F32)
            + jnp.dot(a_lo, b_hi, preferred_element_type=F32))


def _dot_split(a, b, passes, split_lhs=True):
    rem = a if split_lhs else b
    acc = None
    for _ in range(passes):
        piece = rem.astype(BF16)
        part = jnp.dot(piece, b, preferred_element_type=F32) if split_lhs else jnp.dot(
            a, piece, preferred_element_type=F32)
        acc = part if acc is None else acc + part
        rem = rem - piece.astype(F32)
    return acc


def _inproj_kernel(x_ref, wa_ref, wb_ref, wdt_ref, o_ref, dt_ref, xb_ref, *, tiles_a):
    j = pl.program_id(1)

    @pl.when(j == 0)
    def _():
        xb = x_ref[...].astype(BF16)
        xb_ref[...] = xb
        dt_ref[...] = jnp.dot(xb, wdt_ref[...], preferred_element_type=F32)

    @pl.when(j < tiles_a)
    def _():
        o_ref[...] = jnp.dot(xb_ref[...], wa_ref[...], preferred_element_type=F32).astype(o_ref.dtype)

    @pl.when(j >= tiles_a)
    def _():
        o_ref[...] = jnp.dot(xb_ref[...], wb_ref[...], preferred_element_type=F32).astype(o_ref.dtype)


def _inproj(x2, w_a, w_b, w_dt, tm, tn):
    t, d = x2.shape
    tiles_a, tiles_b = w_a.shape[1] // tn, w_b.shape[1] // tn
    n = (tiles_a + tiles_b) * tn
    return pl.pallas_call(
        functools.partial(_inproj_kernel, tiles_a=tiles_a),
        grid=(t // tm, tiles_a + tiles_b),
        in_specs=[
            pl.BlockSpec((tm, d), lambda i, j: (i, 0)),
            pl.BlockSpec((d, tn), lambda i, j: (0, jnp.minimum(j, tiles_a - 1))),
            pl.BlockSpec((d, tn), lambda i, j: (0, jnp.maximum(j - tiles_a, 0))),
            _resident((d, LANES), lambda i, j: (0, 0)),
        ],
        out_specs=[
            pl.BlockSpec((tm, tn), lambda i, j: (i, j)),
            pl.BlockSpec((tm, LANES), lambda i, j: (i, 0)),
        ],
        out_shape=[jax.ShapeDtypeStruct((t, n), BF16), jax.ShapeDtypeStruct((t, LANES), F32)],
        scratch_shapes=[pltpu.VMEM((tm, d), BF16)],
        compiler_params=_cparams(("parallel", "arbitrary")),
        name="inproj0",
    )(x2, w_a, w_b, w_dt)


def _ssd_kernel(z_ref, xs_ref, bm_ref, cm_ref, dt_ref, cw_ref, cb_ref, dtb_ref, aneg_ref, dsk_ref, ng_ref,
                exp_ref, o_ref, xpad_ref, state_ref, act_ref, y_ref, *, n_groups, head_dim):
    c = pl.program_id(1)
    chunk, width = z_ref.shape
    d_state = bm_ref.shape[1] // n_groups
    heads_per_group = width // head_dim // n_groups
    gw = heads_per_group * head_dim

    @pl.when(c == 0)
    def _():
        xpad_ref[0:SUBLANES, :] = jnp.zeros((SUBLANES, xpad_ref.shape[1]), F32)
        state_ref[...] = jnp.zeros_like(state_ref)

    strip = 512
    col0 = 0
    for src in (xs_ref, bm_ref, cm_ref):
        for s0 in range(0, src.shape[1], strip):
            cols = slice(col0 + s0, col0 + s0 + strip)
            xs_ = src[:, s0:s0 + strip].astype(F32)
            xpad_ref[SUBLANES:, cols] = xs_
            acc = cb_ref[0:1, cols] + cw_ref[SSD_CONV - 1:SSD_CONV, cols] * xs_
            for k in range(1, SSD_CONV):
                acc = acc + cw_ref[SSD_CONV - 1 - k:SSD_CONV - k, cols] * xpad_ref[SUBLANES - k:SUBLANES - k + chunk, cols]
            act_ref[:, cols] = _silu(acc)
            xpad_ref[0:SUBLANES, cols] = xs_[chunk - SUBLANES:, :]
        col0 += src.shape[1]

    dt_in = dt_ref[...] + dtb_ref[0:1, :]
    dt = jnp.maximum(dt_in, 0.0) + jnp.log(1.0 + jnp.exp(-jnp.abs(dt_in)))
    a = dt * aneg_ref[0:1, :]
    row = lax.broadcasted_iota(jnp.int32, (chunk, chunk), 0)
    col = lax.broadcasted_iota(jnp.int32, (chunk, chunk), 1)
    causal = row >= col
    cs = _dot_split(jnp.where(causal, 1.0, 0.0).astype(BF16), a, 3, split_lhs=False)
    cs2 = cs * math.log2(math.e)
    cs2_t = cs2.T
    cs_last = cs[chunk - 1:chunk, :]
    expand = exp_ref[...]
    dt_x = _dot_split(dt, expand, 2)
    ecs_x = _dot_split(jnp.exp(cs), expand, 2)
    dec_x = _dot_split(jnp.exp(cs_last - cs), expand, 2)
    sdec_x = _dot_split(jnp.broadcast_to(jnp.exp(cs_last), (SUBLANES, LANES)), expand, 2)[0:1, :]

    lane = lax.broadcasted_iota(jnp.int32, (chunk, LANES), 1)
    first_half = lane < head_dim
    for g in range(n_groups):
        gs = slice(g * gw, (g + 1) * gw)
        b_g = act_ref[:, width + g * d_state:width + (g + 1) * d_state].astype(BF16)
        c_g = act_ref[:, width + (n_groups + g) * d_state:width + (n_groups + g + 1) * d_state].astype(BF16)
        xs_g = act_ref[:, gs]
        xdt = xs_g * dt_x[:, gs]
        xdt_b = xdt.astype(BF16)
        cb = jnp.where(causal, _nt_dot(c_g, b_g), 0.0)
        st = state_ref[:, gs]
        y_g = jnp.dot(c_g, st.astype(BF16), preferred_element_type=F32) * ecs_x[:, gs] + dsk_ref[0:1, gs] * xs_g
        state_ref[:, gs] = st * sdec_x[:, gs] + _tn_dot(b_g, (xdt * dec_x[:, gs]).astype(BF16))
        for pr in range(heads_per_group // 2):
            xp = xdt_b[:, pr * LANES:(pr + 1) * LANES]
            halves = []
            for half in range(2):
                h = g * heads_per_group + 2 * pr + half
                seg = cs2[:, h:h + 1] - cs2_t[h:h + 1, :]
                m = (cb * jnp.exp2(jnp.minimum(seg, 0.0))).astype(BF16)
                halves.append(jnp.dot(m, xp, preferred_element_type=F32))
            y_pair = jnp.where(first_half, halves[0], halves[1])
            y_ref[:, g * gw + pr * LANES:g * gw + (pr + 1) * LANES] = y_pair + y_g[:, pr * LANES:(pr + 1) * LANES]

    yz = y_ref[...] * _silu(z_ref[...].astype(F32))
    ms = jnp.mean(yz * yz, axis=1, keepdims=True)
    o_ref[...] = (yz * lax.rsqrt(ms + RMS_EPS) * ng_ref[0:1, :]).astype(o_ref.dtype)


def _ssd(proj, dt_raw, cw, cb, dtb, aneg, dsk, ng, expand, *, batch, seq, width, n_groups, d_state,
         z_col, xbc_col):
    conv_dim = width + 2 * n_groups * d_state
    bc_w = n_groups * d_state
    nc = seq // SSD_CHUNK
    t = batch * seq
    kern = functools.partial(_ssd_kernel, n_groups=n_groups, head_dim=SSD_HEAD_DIM)
    b_col, c_col = xbc_col + width, xbc_col + width + bc_w
    assert z_col % width == 0 and xbc_col % width == 0 and b_col % bc_w == 0 and c_col % bc_w == 0
    const = lambda b, c: (0, 0)
    return pl.pallas_call(
        kern,
        grid=(batch, nc),
        in_specs=[
            pl.BlockSpec((SSD_CHUNK, width), lambda b, c: (b * nc + c, z_col // width)),
            pl.BlockSpec((SSD_CHUNK, width), lambda b, c: (b * nc + c, xbc_col // width)),
            pl.BlockSpec((SSD_CHUNK, bc_w), lambda b, c: (b * nc + c, b_col // bc_w)),
            pl.BlockSpec((SSD_CHUNK, bc_w), lambda b, c: (b * nc + c, c_col // bc_w)),
            pl.BlockSpec((SSD_CHUNK, LANES), lambda b, c: (b * nc + c, 0)),
            _resident((SUBLANES, conv_dim), const),
            _resident((SUBLANES, conv_dim), const),
            _resident((SUBLANES, LANES), const),
            _resident((SUBLANES, LANES), const),
            _resident((SUBLANES, width), const),
            _resident((SUBLANES, width), const),
            _resident((LANES, width), const),
        ],
        out_specs=pl.BlockSpec((SSD_CHUNK, width), lambda b, c: (b * nc + c, 0)),
        out_shape=jax.ShapeDtypeStruct((t, width), BF16),
        scratch_shapes=[
            pltpu.VMEM((SUBLANES + SSD_CHUNK, conv_dim), F32),
            pltpu.VMEM((d_state, width), F32),
            pltpu.VMEM((SSD_CHUNK, conv_dim), F32),
            pltpu.VMEM((SSD_CHUNK, width), F32),
        ],
        compiler_params=_cparams(("parallel", "arbitrary")),
        name="ssd0",
    )(proj, proj, proj, proj, dt_raw, cw, cb, dtb, aneg, dsk, ng, expand)


def _rope(v, cos, sin):
    half = ROPE_DIM // 2
    lane = lax.broadcasted_iota(jnp.int32, v.shape, 1)
    partner = jnp.where(lane < half, pltpu.roll(v, LANES - half, 1), pltpu.roll(v, half, 1))
    return v * cos + partner * sin


def _moba_kernel(q_ref, k_ref, v_ref, g_ref, cos_ref, sin_ref, o_ref, kr_ref, vt_ref, km_ref, ch_ref,
                 sa_ref, sb_ref, *, n_blocks, heads_per_step):
    qb = pl.program_id(2)
    blk = MOBA_BLOCK
    hd = ATT_HEAD_DIM
    scale = hd ** -0.5 * math.log2(math.e)
    heads = range(heads_per_step)

    n_rows = km_ref.shape[1]

    @pl.when(qb == 0)
    def _():
        km_ref[...] = jnp.zeros_like(km_ref)
        for n in range(n_blocks):
            rows = slice(n * blk, (n + 1) * blk)
            for j in heads:
                cols = slice(j * hd, (j + 1) * hd)
                kr = _rope(k_ref[rows, cols].astype(F32), cos_ref[rows, :], sin_ref[rows, :])
                kr_ref[j, rows, :] = kr.astype(BF16)
                km_ref[j, n + 1:n + 2, :] = jnp.mean(kr, axis=0, keepdims=True)
                vt_ref[j, 0:hd, rows] = v_ref[rows, cols].astype(F32).T.astype(BF16)
                vt_ref[j, hd:, rows] = jnp.ones((vt_ref.shape[1] - hd, blk), BF16)

    q0 = pl.multiple_of(qb * blk, blk)
    cos_q = cos_ref[pl.ds(q0, blk), :]
    sin_q = sin_ref[pl.ds(q0, blk), :]
    row_id = lax.broadcasted_iota(jnp.int32, (n_rows, blk), 0)
    past = (row_id >= 1) & (row_id <= qb)
    key_i = lax.broadcasted_iota(jnp.int32, (blk, blk), 0)
    qry_i = lax.broadcasted_iota(jnp.int32, (blk, blk), 1)

    qs_t = []
    for j in heads:
        qr_t = _rope(q_ref[:, j * hd:(j + 1) * hd].astype(F32), cos_q, sin_q).T
        qs_t.append((qr_t * scale).astype(BF16))

        s_own = jnp.dot(kr_ref[j, pl.ds(q0, blk), :], qs_t[j], preferred_element_type=F32)
        sa_ref[j] = jnp.where(key_i <= qry_i, s_own, NEG_INF)

        gate = jnp.where(past, _dot_hi(km_ref[j], qr_t), NEG_INF)
        rank = jnp.zeros((n_rows, blk), jnp.int32)
        for n2 in range(1, n_blocks):
            other = gate[n2:n2 + 1, :]
            rank = rank + jnp.where(other > gate, 1,
                                    jnp.where(other == gate, jnp.where(row_id > n2, 1, 0), 0))
        ch_ref[j] = jnp.where(row_id == 0, 1.0, jnp.where(past, jnp.where(rank < MOBA_TOPK, 1.0, 0.0), 0.0))

    def stage(k, ref):
        n0 = pl.multiple_of(jnp.clip(k - 1, 0, n_blocks - 1) * blk, blk)
        for j in heads:
            ref[j] = jnp.dot(kr_ref[j, pl.ds(n0, blk), :], qs_t[j], preferred_element_type=F32)

    def consume(k, ref, carry):
        n0 = pl.multiple_of(jnp.where(k == 0, qb, k - 1) * blk, blk)
        out = []
        for j in heads:
            m, acc = carry[j]
            taken = ch_ref[j, pl.ds(k, 1), :] > 0.5
            m_new = jnp.maximum(m, jnp.where(taken, jnp.max(ref[j], axis=0, keepdims=True), NEG_INF))
            alpha = jnp.exp2(m - m_new)
            p = jnp.exp2(ref[j] - jnp.where(taken, m_new, -NEG_INF)).astype(BF16)
            out.append((m_new, alpha * acc + jnp.dot(vt_ref[j, :, pl.ds(n0, blk)], p,
                                                     preferred_element_type=F32)))
        return tuple(out)

    def two_blocks(i, carry):
        stage(2 * i + 1, sb_ref)
        carry = consume(2 * i, sa_ref, carry)
        stage(2 * i + 2, sa_ref)
        return consume(2 * i + 1, sb_ref, carry)

    init = tuple((jnp.full((1, blk), NEG_INF, F32), jnp.zeros((vt_ref.shape[1], blk), F32)) for _ in heads)
    final = lax.fori_loop(0, qb // 2 + 1, two_blocks, init)
    for j in heads:
        _, acc = final[j]
        cols = slice(j * hd, (j + 1) * hd)
        out_t = acc[0:hd, :] / acc[hd:hd + 1, :]
        o_ref[:, cols] = (out_t.T * _silu(g_ref[:, cols].astype(F32))).astype(o_ref.dtype)


def _moba(proj, cos_t, sin_t, *, batch, seq, heads, q_col, k_col, v_col, g_col, heads_per_step):
    nb = seq // MOBA_BLOCK
    hd = ATT_HEAD_DIM
    hw = hd * heads_per_step
    t = batch * seq
    assert heads % heads_per_step == 0 and all(c % hw == 0 for c in (q_col, k_col, v_col, g_col))
    qc, kc, vc, gc = (c // hw for c in (q_col, k_col, v_col, g_col))
    kern = functools.partial(_moba_kernel, n_blocks=nb, heads_per_step=heads_per_step)
    return pl.pallas_call(
        kern,
        grid=(batch, heads // heads_per_step, nb),
        in_specs=[
            pl.BlockSpec((MOBA_BLOCK, hw), lambda b, h, i: (b * nb + i, qc + h)),
            pl.BlockSpec((seq, hw), lambda b, h, i: (b, kc + h)),
            pl.BlockSpec((seq, hw), lambda b, h, i: (b, vc + h)),
            pl.BlockSpec((MOBA_BLOCK, hw), lambda b, h, i: (b * nb + i, gc + h)),
            _resident((seq, hd), lambda b, h, i: (0, 0)),
            _resident((seq, hd), lambda b, h, i: (0, 0)),
        ],
        out_specs=pl.BlockSpec((MOBA_BLOCK, hw), lambda b, h, i: (b * nb + i, h)),
        out_shape=jax.ShapeDtypeStruct((t, heads * hd), BF16),
        scratch_shapes=[
            pltpu.VMEM((heads_per_step, seq, hd), BF16),
            pltpu.VMEM((heads_per_step, hd + 16, seq), BF16),
            pltpu.VMEM((heads_per_step, nb + SUBLANES, hd), F32),
            pltpu.VMEM((heads_per_step, nb + SUBLANES, MOBA_BLOCK), F32),
            pltpu.VMEM((heads_per_step, MOBA_BLOCK, MOBA_BLOCK), F32),
            pltpu.VMEM((heads_per_step, MOBA_BLOCK, MOBA_BLOCK), F32),
        ],
        compiler_params=_cparams(("parallel", "parallel", "arbitrary")),
        name="moba0",
    )(proj, proj, proj, proj, cos_t, sin_t)


def _out0_kernel(ya_ref, yb_ref, x_ref, w_ref, g_ref, b_ref, o_ref, *, alpha):
    ka = ya_ref.shape[1]
    h = jnp.dot(ya_ref[...], w_ref[0:ka, :], preferred_element_type=F32)
    h = h + jnp.dot(yb_ref[...], w_ref[ka:, :], preferred_element_type=F32)
    v = alpha * x_ref[...] + h
    mu = jnp.mean(v, axis=1, keepdims=True)
    vc = v - mu
    var = jnp.mean(vc * vc, axis=1, keepdims=True)
    o_ref[...] = vc * lax.rsqrt(var + LN_EPS) * g_ref[0:1, :] + b_ref[0:1, :]


def _out0(ya, yb, x2, w, g, b, *, alpha, tm):
    t, d = x2.shape
    ka, kb = ya.shape[1], yb.shape[1]
    return pl.pallas_call(
        functools.partial(_out0_kernel, alpha=alpha),
        grid=(t // tm,),
        in_specs=[
            pl.BlockSpec((tm, ka), lambda i: (i, 0)),
            pl.BlockSpec((tm, kb), lambda i: (i, 0)),
            pl.BlockSpec((tm, d), lambda i: (i, 0)),
            _resident((ka + kb, d), lambda i: (0, 0)),
            _resident((SUBLANES, d), lambda i: (0, 0)),
            _resident((SUBLANES, d), lambda i: (0, 0)),
        ],
        out_specs=pl.BlockSpec((tm, d), lambda i: (i, 0)),
        out_shape=jax.ShapeDtypeStruct((t, d), F32),
        compiler_params=_cparams(("parallel",)),
        name="out0",
    )(ya, yb, x2, w, g, b)


def _inproj1_kernel(x_ref, w_ref, o_ref):
    o_ref[0] = jnp.dot(w_ref[...], x_ref[0].astype(BF16), preferred_element_type=F32).astype(o_ref.dtype)


def _inproj1(xt, wt):
    l, d, r = xt.shape
    n = wt.shape[0]
    return pl.pallas_call(
        _inproj1_kernel,
        grid=(l,),
        in_specs=[pl.BlockSpec((1, d, r), lambda i: (i, 0, 0)), _resident((n, d), lambda i: (0, 0))],
        out_specs=pl.BlockSpec((1, n, r), lambda i: (i, 0, 0)),
        out_shape=jax.ShapeDtypeStruct((l, n, r), BF16),
        compiler_params=_cparams(("parallel",)),
        name="inproj1",
    )(xt, wt)


def _cpow(lr, li, d, nbits):
    pr = jnp.ones(d.shape, F32)
    pi = jnp.zeros(d.shape, F32)
    br, bi = lr, li
    for bit in range(nbits):
        on = ((d >> bit) & 1) == 1
        nr = pr * br - pi * bi
        ni = pr * bi + pi * br
        pr = jnp.where(on, nr, pr)
        pi = jnp.where(on, ni, pi)
        if bit + 1 < nbits:
            br, bi = br * br - bi * bi, 2.0 * br * bi
    return pr, pi


def _gelu_tanh(v):
    return 0.5 * v * (1.0 + jnp.tanh(math.sqrt(2.0 / math.pi) * (v + 0.044715 * (v * v * v))))


def _s5_group(j, u_ref, lamc_ref, lamr_ref, bb_ref, rep_ref, cc_ref, ca_ref, cbm_ref, dsk_ref, o_ref, tz_ref,
              cl_ref, *, chunk, chunks_per_seq):
    m = S5_GROUP
    n = S5_STATE
    lm = chunk * m
    r = u_ref.shape[2]
    u = u_ref[:, j * m:(j + 1) * m, :].reshape(lm, r)

    pos_per_blk = LANES // m
    n_lane_blk = lm // LANES
    lam_r, lam_i = lamc_ref[j, 0], lamc_ref[j, 1]
    lane = lax.broadcasted_iota(jnp.int32, (n, LANES), 1)
    pr, pi = _cpow(lam_r, lam_i, (pos_per_blk - 1) - lane // m, (pos_per_blk - 1).bit_length())
    lbr, lbi = lam_r, lam_i
    for _ in range(pos_per_blk.bit_length() - 1):
        lbr, lbi = lbr * lbr - lbi * lbi, 2.0 * lbr * lbi
    bbr = _dot_split(bb_ref[j, 0], rep_ref[...], 3)
    bbi = _dot_split(bb_ref[j, 1], rep_ref[...], 3)
    bl_r, bl_i = [None] * n_lane_blk, [None] * n_lane_blk
    for c in range(n_lane_blk - 1, -1, -1):
        bl_r[c] = pr * bbr - pi * bbi
        bl_i[c] = pr * bbi + pi * bbr
        pr, pi = pr * lbr - pi * lbi, pr * lbi + pi * lbr
    bl = jnp.concatenate([jnp.concatenate(bl_r, axis=1), jnp.concatenate(bl_i, axis=1)], axis=0)

    krev = _dot_hi(cc_ref[j], bl)

    col_s = lax.broadcasted_iota(jnp.int32, (m, lm), 1) // m
    for t in range(chunk):
        rolled = pltpu.roll(krev, ((t + 1) * m) % lm, 1)
        tz_ref[j, t * m:(t + 1) * m, :] = jnp.where(col_s <= t, rolled, 0.0).astype(BF16)

    y = jnp.dot(tz_ref[j], u, preferred_element_type=F32)

    e = jnp.dot(bl.astype(BF16), u, preferred_element_type=F32)
    lane_r = lax.broadcasted_iota(jnp.int32, (n, r), 1) % chunks_per_seq
    hr = jnp.where(lane_r >= 1, pltpu.roll(e[:n], 1, 1), 0.0)
    hi = jnp.where(lane_r >= 1, pltpu.roll(e[n:], 1, 1), 0.0)
    mr, mi = lbr, lbi
    for _ in range((chunk // pos_per_blk).bit_length() - 1):
        mr, mi = mr * mr - mi * mi, 2.0 * mr * mi
    mr = jnp.concatenate([mr] * (r // LANES), axis=1) if r > LANES else mr[:, :r]
    mi = jnp.concatenate([mi] * (r // LANES), axis=1) if r > LANES else mi[:, :r]
    step = 1
    while step < chunks_per_seq:
        sr = jnp.where(lane_r >= step, pltpu.roll(hr, step, 1), 0.0)
        si = jnp.where(lane_r >= step, pltpu.roll(hi, step, 1), 0.0)
        hr, hi = hr + mr * sr - mi * si, hi + mr * si + mi * sr
        mr, mi = mr * mr - mi * mi, 2.0 * mr * mi
        step *= 2
    h_in = jnp.concatenate([hr, hi], axis=0).astype(BF16)

    lam_rr = jnp.broadcast_to(lamr_ref[j, 0:1, :], (m, 2 * n))
    lam_ri = jnp.broadcast_to(lamr_ref[j, 1:2, :], (m, 2 * n))
    ca, cbm = ca_ref[j], cbm_ref[j]
    qr, qi = lam_rr, lam_ri
    for t in range(chunk):
        cl_ref[j, t * m:(t + 1) * m, :] = (ca * qr + cbm * qi).astype(BF16)
        if t + 1 < chunk:
            qr, qi = qr * lam_rr - qi * lam_ri, qr * lam_ri + qi * lam_rr
    y = y + jnp.dot(cl_ref[j], h_in, preferred_element_type=F32)

    dsk = jnp.concatenate([dsk_ref[j]] * chunk, axis=0)
    dsk = jnp.concatenate([dsk] * (r // LANES), axis=1) if r > LANES else dsk[:, :r]
    y = y + dsk * u.astype(F32)
    o_ref[:, j * m:(j + 1) * m, :] = _gelu_tanh(y).astype(o_ref.dtype).reshape(chunk, m, r)


def _s5_kernel(*refs, chunk, chunks_per_seq, groups_per_step):
    for j in range(groups_per_step):
        _s5_group(j, *refs, chunk=chunk, chunks_per_seq=chunks_per_seq)


def _s5(ug, lamc, lamr, bb, rep, cc, ca, cbm, dsk, *, chunks_per_seq):
    chunk, _, r = ug.shape
    g = lamc.shape[0]
    m, n = S5_GROUP, S5_STATE
    lm = chunk * m
    gps = 2
    kern = functools.partial(_s5_kernel, chunk=chunk, chunks_per_seq=chunks_per_seq, groups_per_step=gps)
    p4 = lambda i: (i, 0, 0, 0)
    p3 = lambda i: (i, 0, 0)
    return pl.pallas_call(
        kern,
        grid=(g // gps,),
        in_specs=[
            pl.BlockSpec((chunk, gps * m, r), lambda i: (0, i, 0)),
            pl.BlockSpec((gps, 2, n, LANES), p4),
            pl.BlockSpec((gps, SUBLANES, 2 * n), p3),
            pl.BlockSpec((gps, 2, n, m), p4),
            _resident((m, LANES), lambda i: (0, 0)),
            pl.BlockSpec((gps, m, 2 * n), p3),
            pl.BlockSpec((gps, m, 2 * n), p3),
            pl.BlockSpec((gps, m, 2 * n), p3),
            pl.BlockSpec((gps, m, LANES), p3),
        ],
        out_specs=pl.BlockSpec((chunk, gps * m, r), lambda i: (0, i, 0)),
        out_shape=jax.ShapeDtypeStruct((chunk, g * m, r), BF16),
        scratch_shapes=[pltpu.VMEM((gps, lm, lm), BF16), pltpu.VMEM((gps, lm, 2 * n), BF16)],
        compiler_params=_cparams(("parallel",)),
        name="s5scan",
    )(ug, lamc, lamr, bb, rep, cc, ca, cbm, dsk)


def _out1_kernel(y_ref, gate_ref, x_ref, wg_ref, wo_ref, g_ref, b_ref, o_ref, v_ref, s_ref, *, alpha):
    w = y_ref.shape[1]
    d = x_ref.shape[1]
    mc = 512
    y = y_ref[0]
    for c0 in range(0, w, mc):
        ga = jnp.dot(wg_ref[c0:c0 + mc, :], y, preferred_element_type=F32)
        gb = jnp.dot(wg_ref[w + c0:w + c0 + mc, :], y, preferred_element_type=F32)
        v_ref[c0:c0 + mc, :] = (ga * _sigmoid(gb) * _silu(gate_ref[0, c0:c0 + mc, :].astype(F32))).astype(BF16)
    v = v_ref[...]
    for c0 in range(0, d, mc):
        s_ref[c0:c0 + mc, :] = alpha * x_ref[0, c0:c0 + mc, :] + jnp.dot(
            wo_ref[c0:c0 + mc, :], v, preferred_element_type=F32)
    s = s_ref[...]
    mu = jnp.mean(s, axis=0, keepdims=True)
    sc = s - mu
    var = jnp.mean(sc * sc, axis=0, keepdims=True)
    o_ref[0] = sc * lax.rsqrt(var + LN_EPS) * g_ref[...] + b_ref[...]


def _out1(y3, ug, xt, wgt, wot, g_col, b_col, *, alpha):
    l, w, r = y3.shape
    d = xt.shape[1]
    gate_blk = 1
    return pl.pallas_call(
        functools.partial(_out1_kernel, alpha=alpha),
        grid=(l,),
        in_specs=[
            pl.BlockSpec((1, w, r), lambda i: (i, 0, 0)),
            pl.BlockSpec((1, w, r), lambda i: (i, gate_blk, 0)),
            pl.BlockSpec((1, d, r), lambda i: (i, 0, 0)),
            _resident((2 * w, w), lambda i: (0, 0)),
            _resident((d, w), lambda i: (0, 0)),
            _resident((d, 1), lambda i: (0, 0)),
            _resident((d, 1), lambda i: (0, 0)),
        ],
        out_specs=pl.BlockSpec((1, d, r), lambda i: (i, 0, 0)),
        out_shape=jax.ShapeDtypeStruct((l, d, r), F32),
        scratch_shapes=[pltpu.VMEM((w, r), BF16), pltpu.VMEM((d, r), F32)],
        compiler_params=_cparams(("parallel",)),
        name="out1",
    )(y3, ug, xt, wgt, wot, g_col, b_col)


def _pad_rows(v, rows=SUBLANES):
    v = jnp.atleast_2d(v.astype(F32))
    return jnp.pad(v, ((0, rows - v.shape[0]), (0, 0)))


def _pad_lanes(v, lanes=LANES):
    return jnp.pad(v, [(0, 0)] * (v.ndim - 1) + [(0, lanes - v.shape[-1])])


def _layer0(x2, batch, seq, in_w, conv_w, conv_b, dt_bias, a_log, d_skip, norm_g, out_w, ln_g, ln_b, alpha):
    d = x2.shape[1]
    ssd_heads = dt_bias.shape[0]
    width = ssd_heads * SSD_HEAD_DIM
    conv_dim = conv_w.shape[1]
    d_state = (conv_dim - width) // (2 * SSD_GROUPS)
    att_width = (in_w.shape[1] - width - conv_dim - ssd_heads) // 4
    att_heads = att_width // ATT_HEAD_DIM

    z_end = width
    xbc_end = z_end + conv_dim
    dt_end = xbc_end + ssd_heads
    w_a = in_w[:, :xbc_end].astype(BF16)
    w_b = in_w[:, dt_end:].astype(BF16)
    w_dt = _pad_lanes(in_w[:, xbc_end:dt_end]).astype(BF16)
    z_col, xbc_col = 0, z_end
    q_col = xbc_end
    k_col, v_col, g_col = q_col + att_width, q_col + 2 * att_width, q_col + 3 * att_width

    proj, dt_raw = _inproj(x2, w_a, w_b, w_dt, tm=1024, tn=1024)

    expand = (jnp.arange(LANES)[:, None] == (jnp.arange(width)[None, :] // SSD_HEAD_DIM)).astype(BF16)
    ya = _ssd(proj, dt_raw, _pad_rows(conv_w), _pad_rows(conv_b), _pad_rows(_pad_lanes(dt_bias.astype(F32))),
              _pad_rows(_pad_lanes(-jnp.exp(a_log.astype(F32)))),
              _pad_rows(jnp.repeat(d_skip.astype(F32), SSD_HEAD_DIM)), _pad_rows(norm_g), expand,
              batch=batch, seq=seq, width=width, n_groups=SSD_GROUPS, d_state=d_state,
              z_col=z_col, xbc_col=xbc_col)

    half = ROPE_DIM // 2
    inv_freq = ROPE_THETA ** (-(jnp.arange(half, dtype=F32) * 2.0 / ROPE_DIM))
    ang = jnp.arange(seq, dtype=F32)[:, None] * inv_freq[None, :]
    ones = jnp.ones((seq, ATT_HEAD_DIM - ROPE_DIM), F32)
    cos_t = jnp.concatenate([jnp.cos(ang), jnp.cos(ang), ones], axis=1)
    sin_t = jnp.concatenate([-jnp.sin(ang), jnp.sin(ang), 0.0 * ones], axis=1)
    yb = _moba(proj, cos_t, sin_t, batch=batch, seq=seq, heads=att_heads,
               q_col=q_col, k_col=k_col, v_col=v_col, g_col=g_col, heads_per_step=4)

    return _out0(ya, yb, x2, out_w.astype(BF16), _pad_rows(ln_g), _pad_rows(ln_b), alpha=alpha, tm=512)


def _layer1(x2, batch, seq, in_w, lam_re, lam_im, log_dt, b_re, b_im, c_re, c_im, d_skip, glu_w, out_w,
            ln_g, ln_b, alpha):
    t, d = x2.shape
    chunk = S5_L
    cps = seq // chunk
    r = batch * cps
    groups, n = lam_re.shape
    m = S5_GROUP
    w = groups * m

    lre, lim = lam_re.astype(F32), lam_im.astype(F32)
    dt = jnp.exp(log_dt.astype(F32))[:, None]
    mag = jnp.exp(lre * dt)
    lbr, lbi = mag * jnp.cos(lim * dt), mag * jnp.sin(lim * dt)
    den = lre * lre + lim * lim
    fr = ((lbr - 1.0) * lre + lbi * lim) / den
    fi = (lbi * lre - (lbr - 1.0) * lim) / den
    bre, bim = b_re.astype(F32), b_im.astype(F32)
    bbr = fr[..., None] * bre - fi[..., None] * bim
    bbi = fr[..., None] * bim + fi[..., None] * bre
    lamc = jnp.broadcast_to(jnp.stack([lbr, lbi], axis=1)[..., None], (groups, 2, n, LANES))
    lamr = jnp.stack([jnp.concatenate([lbr, lbr], -1), jnp.concatenate([lbi, lbi], -1)], axis=1)
    lamr = jnp.pad(lamr, ((0, 0), (0, SUBLANES - 2), (0, 0)))
    bb = jnp.stack([bbr, bbi], axis=1)
    rep = (jnp.arange(m)[:, None] == (jnp.arange(LANES)[None, :] % m)).astype(BF16)
    cr, ci = c_re.astype(F32), c_im.astype(F32)
    cc = jnp.concatenate([cr, -ci], axis=-1)
    cbm = jnp.concatenate([-ci, -cr], axis=-1)
    dsk = jnp.broadcast_to(d_skip.astype(F32).reshape(groups, m, 1), (groups, m, LANES))

    xt = x2.reshape(r, chunk, d).transpose(1, 2, 0)
    ug = _inproj1(xt, in_w.T.astype(BF16))
    y3 = _s5(ug, lamc, lamr, bb, rep, cc, cc, cbm, dsk, chunks_per_seq=cps)
    o3 = _out1(y3, ug, xt, glu_w.T.astype(BF16), out_w.T.astype(BF16),
               ln_g.astype(F32)[:, None], ln_b.astype(F32)[:, None], alpha=alpha)
    return o3.transpose(2, 0, 1).reshape(t, d)


def kernel(x, in0_w, conv_w, conv_b, dt_bias, a_log, ssd_d, ssd_norm_g, out0_w, in1_w, s5_lam_re, s5_lam_im,
           s5_log_dt, s5_b_re, s5_b_im, s5_c_re, s5_c_im, s5_d, glu_w, out1_w, ln_g, ln_b):
    batch, seq, d = x.shape
    depth = ln_g.shape[0]
    alpha = (2 * depth) ** 0.25
    x2 = x.reshape(batch * seq, d)
    for layer in range(depth):
        i = layer // 2
        if layer % 2 == 0:
            x2 = _layer0(x2, batch, seq, in0_w[i], conv_w[i], conv_b[i], dt_bias[i], a_log[i], ssd_d[i],
                         ssd_norm_g[i], out0_w[i], ln_g[layer], ln_b[layer], alpha)
        else:
            x2 = _layer1(x2, batch, seq, in1_w[i], s5_lam_re[i], s5_lam_im[i], s5_log_dt[i], s5_b_re[i],
                         s5_b_im[i], s5_c_re[i], s5_c_im[i], s5_d[i], glu_w[i], out1_w[i],
                         ln_g[layer], ln_b[layer], alpha)
    return x2.reshape(batch, seq, d).astype(x.dtype)
```

```python
import functools
import math

import jax
import jax.numpy as jnp
from jax import lax
from jax.experimental import pallas as pl
from jax.experimental.pallas import tpu as pltpu

F32 = jnp.float32
BF16 = jnp.bfloat16

SSD_HEAD_DIM = 64
SSD_GROUPS = 4
SSD_STATE = 128
SSD_CONV = 4
SSD_CHUNK = 256
ATT_HEAD_DIM = 128
MOBA_BLOCK = 256
MOBA_TOPK = 3
ROPE_THETA = 500000.0
ROPE_DIM = ATT_HEAD_DIM // 4
S5_GROUP = 16
S5_STATE = 64
LN_EPS = 1e-5
RMS_EPS = 1e-5
NEG_INF = -1e30

LANES = 128
SUBLANES = 8
VMEM_LIMIT = 56 * 1024 * 1024

S5_L = 64


def _cparams(sem):
    return pltpu.CompilerParams(dimension_semantics=sem, vmem_limit_bytes=VMEM_LIMIT)


def _resident(shape, index_map):
    return pl.BlockSpec(shape, index_map, pipeline_mode=pl.Buffered(1))


def _sigmoid(v):
    return 0.5 + 0.5 * jnp.tanh(0.5 * v)


def _silu(v):
    h = 0.5 * v
    return h + h * jnp.tanh(h)


def _nt_dot(a, b):
    return lax.dot_general(a, b, (((1,), (1,)), ((), ())), preferred_element_type=F32)


def _tn_dot(a, b):
    return lax.dot_general(a, b, (((0,), (0,)), ((), ())), preferred_element_type=F32)


def _dot_hi(a, b):
    a_hi = a.astype(BF16)
    b_hi = b.astype(BF16)
    a_lo = (a - a_hi.astype(F32)).astype(BF16)
    b_lo = (b - b_hi.astype(F32)).astype(BF16)
    return (jnp.dot(a_hi, b_hi, preferred_element_type=F32) + jnp.dot(a_hi, b_lo, preferred_element_type=F32)
            + jnp.dot(a_lo, b_hi, preferred_element_type=F32))


def _dot_split(a, b, passes, split_lhs=True):
    rem = a if split_lhs else b
    acc = None
    for _ in range(passes):
        piece = rem.astype(BF16)
        part = jnp.dot(piece, b, preferred_element_type=F32) if split_lhs else jnp.dot(
            a, piece, preferred_element_type=F32)
        acc = part if acc is None else acc + part
        rem = rem - piece.astype(F32)
    return acc


def _inproj_kernel(x_ref, wa_ref, wb_ref, wdt_ref, o_ref, dt_ref, xb_ref, *, tiles_a):
    j = pl.program_id(1)

    @pl.when(j == 0)
    def _():
        xb = x_ref[...].astype(BF16)
        xb_ref[...] = xb
        dt_ref[...] = jnp.dot(xb, wdt_ref[...], preferred_element_type=F32)

    @pl.when(j < tiles_a)
    def _():
        o_ref[...] = jnp.dot(xb_ref[...], wa_ref[...], preferred_element_type=F32).astype(o_ref.dtype)

    @pl.when(j >= tiles_a)
    def _():
        o_ref[...] = jnp.dot(xb_ref[...], wb_ref[...], preferred_element_type=F32).astype(o_ref.dtype)


def _inproj(x2, w_a, cols_a, w_b, w_dt, tm, tn):
    t, d = x2.shape
    assert cols_a % tn == 0 and w_b.shape[1] % tn == 0
    tiles_a, tiles_b = cols_a // tn, w_b.shape[1] // tn
    n = (tiles_a + tiles_b) * tn
    return pl.pallas_call(
        functools.partial(_inproj_kernel, tiles_a=tiles_a),
        grid=(t // tm, tiles_a + tiles_b),
        in_specs=[
            pl.BlockSpec((tm, d), lambda i, j: (i, 0)),
            pl.BlockSpec((d, tn), lambda i, j: (0, jnp.minimum(j, tiles_a - 1))),
            pl.BlockSpec((d, tn), lambda i, j: (0, jnp.maximum(j - tiles_a, 0))),
            _resident((d, LANES), lambda i, j: (0, 0)),
        ],
        out_specs=[
            pl.BlockSpec((tm, tn), lambda i, j: (i, j)),
            pl.BlockSpec((tm, LANES), lambda i, j: (i, 0)),
        ],
        out_shape=[jax.ShapeDtypeStruct((t, n), BF16), jax.ShapeDtypeStruct((t, LANES), F32)],
        scratch_shapes=[pltpu.VMEM((tm, d), BF16)],
        compiler_params=_cparams(("parallel", "arbitrary")),
        name="inproj0",
    )(x2, w_a, w_b, w_dt)


def _ssd_kernel(z_ref, xs_ref, bm_ref, cm_ref, dt_ref, cw_ref, cb_ref, dtb_ref, aneg_ref, dsk_ref, ng_ref,
                exp_ref, o_ref, xpad_ref, state_ref, act_ref, y_ref, *, n_groups, head_dim):
    c = pl.program_id(1)
    chunk, width = z_ref.shape
    d_state = bm_ref.shape[1] // n_groups
    heads_per_group = width // head_dim // n_groups
    gw = heads_per_group * head_dim

    @pl.when(c == 0)
    def _():
        xpad_ref[0:SUBLANES, :] = jnp.zeros((SUBLANES, xpad_ref.shape[1]), F32)
        state_ref[...] = jnp.zeros_like(state_ref)

    strip = 512
    col0 = 0
    for src in (xs_ref, bm_ref, cm_ref):
        for s0 in range(0, src.shape[1], strip):
            cols = slice(col0 + s0, col0 + s0 + strip)
            xs_ = src[:, s0:s0 + strip].astype(F32)
            xpad_ref[SUBLANES:, cols] = xs_
            acc = cb_ref[0:1, cols] + cw_ref[SSD_CONV - 1:SSD_CONV, cols] * xs_
            for k in range(1, SSD_CONV):
                acc = acc + cw_ref[SSD_CONV - 1 - k:SSD_CONV - k, cols] * xpad_ref[SUBLANES - k:SUBLANES - k + chunk, cols]
            act_ref[:, cols] = _silu(acc)
            xpad_ref[0:SUBLANES, cols] = xs_[chunk - SUBLANES:, :]
        col0 += src.shape[1]

    dt_in = dt_ref[...] + dtb_ref[0:1, :]
    dt = jnp.maximum(dt_in, 0.0) + jnp.log(1.0 + jnp.exp(-jnp.abs(dt_in)))
    a = dt * aneg_ref[0:1, :]
    row = lax.broadcasted_iota(jnp.int32, (chunk, chunk), 0)
    col = lax.broadcasted_iota(jnp.int32, (chunk, chunk), 1)
    causal = row >= col
    cs = _dot_split(jnp.where(causal, 1.0, 0.0).astype(BF16), a, 3, split_lhs=False)
    cs2 = cs * math.log2(math.e)
    cs2_t = cs2.T
    cs_last = cs[chunk - 1:chunk, :]
    expand = exp_ref[...]
    dt_x = _dot_split(dt, expand, 2)
    ecs_x = _dot_split(jnp.exp(cs), expand, 2)
    dec_x = _dot_split(jnp.exp(cs_last - cs), expand, 2)
    sdec_x = _dot_split(jnp.broadcast_to(jnp.exp(cs_last), (SUBLANES, LANES)), expand, 2)[0:1, :]

    lane = lax.broadcasted_iota(jnp.int32, (chunk, LANES), 1)
    first_half = lane < head_dim
    for g in range(n_groups):
        gs = slice(g * gw, (g + 1) * gw)
        b_g = act_ref[:, width + g * d_state:width + (g + 1) * d_state].astype(BF16)
        c_g = act_ref[:, width + (n_groups + g) * d_state:width + (n_groups + g + 1) * d_state].astype(BF16)
        xs_g = act_ref[:, gs]
        xdt = xs_g * dt_x[:, gs]
        xdt_b = xdt.astype(BF16)
        cb = jnp.where(causal, _nt_dot(c_g, b_g), 0.0)
        st = state_ref[:, gs]
        y_g = jnp.dot(c_g, st.astype(BF16), preferred_element_type=F32) * ecs_x[:, gs] + dsk_ref[0:1, gs] * xs_g
        state_ref[:, gs] = st * sdec_x[:, gs] + _tn_dot(b_g, (xdt * dec_x[:, gs]).astype(BF16))
        for pr in range(heads_per_group // 2):
            xp = xdt_b[:, pr * LANES:(pr + 1) * LANES]
            halves = []
            for half in range(2):
                h = g * heads_per_group + 2 * pr + half
                seg = cs2[:, h:h + 1] - cs2_t[h:h + 1, :]
                m = (cb * jnp.exp2(jnp.minimum(seg, 0.0))).astype(BF16)
                halves.append(jnp.dot(m, xp, preferred_element_type=F32))
            y_pair = jnp.where(first_half, halves[0], halves[1])
            y_ref[:, g * gw + pr * LANES:g * gw + (pr + 1) * LANES] = y_pair + y_g[:, pr * LANES:(pr + 1) * LANES]

    yz = y_ref[...] * _silu(z_ref[...].astype(F32))
    ms = jnp.mean(yz * yz, axis=1, keepdims=True)
    o_ref[...] = (yz * lax.rsqrt(ms + RMS_EPS) * ng_ref[0:1, :]).astype(o_ref.dtype)


def _ssd(proj, dt_raw, cw, cb, dtb, aneg, dsk, ng, expand, *, batch, seq, width, n_groups, d_state,
         z_col, xbc_col):
    conv_dim = width + 2 * n_groups * d_state
    bc_w = n_groups * d_state
    nc = seq // SSD_CHUNK
    t = batch * seq
    kern = functools.partial(_ssd_kernel, n_groups=n_groups, head_dim=SSD_HEAD_DIM)
    b_col, c_col = xbc_col + width, xbc_col + width + bc_w
    assert z_col % width == 0 and xbc_col % width == 0 and b_col % bc_w == 0 and c_col % bc_w == 0
    const = lambda b, c: (0, 0)
    return pl.pallas_call(
        kern,
        grid=(batch, nc),
        in_specs=[
            pl.BlockSpec((SSD_CHUNK, width), lambda b, c: (b * nc + c, z_col // width)),
            pl.BlockSpec((SSD_CHUNK, width), lambda b, c: (b * nc + c, xbc_col // width)),
            pl.BlockSpec((SSD_CHUNK, bc_w), lambda b, c: (b * nc + c, b_col // bc_w)),
            pl.BlockSpec((SSD_CHUNK, bc_w), lambda b, c: (b * nc + c, c_col // bc_w)),
            pl.BlockSpec((SSD_CHUNK, LANES), lambda b, c: (b * nc + c, 0)),
            _resident((SUBLANES, conv_dim), const),
            _resident((SUBLANES, conv_dim), const),
            _resident((SUBLANES, LANES), const),
            _resident((SUBLANES, LANES), const),
            _resident((SUBLANES, width), const),
            _resident((SUBLANES, width), const),
            _resident((LANES, width), const),
        ],
        out_specs=pl.BlockSpec((SSD_CHUNK, width), lambda b, c: (b * nc + c, 0)),
        out_shape=jax.ShapeDtypeStruct((t, width), BF16),
        scratch_shapes=[
            pltpu.VMEM((SUBLANES + SSD_CHUNK, conv_dim), F32),
            pltpu.VMEM((d_state, width), F32),
            pltpu.VMEM((SSD_CHUNK, conv_dim), F32),
            pltpu.VMEM((SSD_CHUNK, width), F32),
        ],
        compiler_params=_cparams(("parallel", "arbitrary")),
        name="ssd0",
    )(proj, proj, proj, proj, dt_raw, cw, cb, dtb, aneg, dsk, ng, expand)


def _rope(v, cos, sin):
    half = ROPE_DIM // 2
    lane = lax.broadcasted_iota(jnp.int32, v.shape, 1)
    partner = jnp.where(lane < half, pltpu.roll(v, LANES - half, 1), pltpu.roll(v, half, 1))
    return v * cos + partner * sin


def _moba_kernel(q_ref, k_ref, v_ref, g_ref, cos_ref, sin_ref, o_ref, kr_ref, vt_ref, km_ref, ch_ref,
                 sa_ref, sb_ref, *, n_blocks, heads_per_step):
    qb = pl.program_id(2)
    blk = MOBA_BLOCK
    hd = ATT_HEAD_DIM
    scale = hd ** -0.5 * math.log2(math.e)
    heads = range(heads_per_step)

    n_rows = km_ref.shape[1]

    @pl.when(qb == 0)
    def _():
        km_ref[...] = jnp.zeros_like(km_ref)
        for n in range(n_blocks):
            rows = slice(n * blk, (n + 1) * blk)
            for j in heads:
                cols = slice(j * hd, (j + 1) * hd)
                kr = _rope(k_ref[rows, cols].astype(F32), cos_ref[rows, :], sin_ref[rows, :])
                kr_ref[j, rows, :] = kr.astype(BF16)
                km_ref[j, n + 1:n + 2, :] = jnp.mean(kr, axis=0, keepdims=True)
                vt_ref[j, 0:hd, rows] = v_ref[rows, cols].astype(F32).T.astype(BF16)
                vt_ref[j, hd:, rows] = jnp.ones((vt_ref.shape[1] - hd, blk), BF16)

    q0 = pl.multiple_of(qb * blk, blk)
    cos_q = cos_ref[pl.ds(q0, blk), :]
    sin_q = sin_ref[pl.ds(q0, blk), :]
    row_id = lax.broadcasted_iota(jnp.int32, (n_rows, blk), 0)
    past = (row_id >= 1) & (row_id <= qb)
    key_i = lax.broadcasted_iota(jnp.int32, (blk, blk), 0)
    qry_i = lax.broadcasted_iota(jnp.int32, (blk, blk), 1)

    qs_t = []
    for j in heads:
        qr_t = _rope(q_ref[:, j * hd:(j + 1) * hd].astype(F32), cos_q, sin_q).T
        qs_t.append((qr_t * scale).astype(BF16))

        s_own = jnp.dot(kr_ref[j, pl.ds(q0, blk), :], qs_t[j], preferred_element_type=F32)
        sa_ref[j] = jnp.where(key_i <= qry_i, s_own, NEG_INF)

        gate = jnp.where(past, _dot_hi(km_ref[j], qr_t), NEG_INF)
        rank = jnp.zeros((n_rows, blk), jnp.int32)
        for n2 in range(1, n_blocks):
            other = gate[n2:n2 + 1, :]
            rank = rank + jnp.where(other > gate, 1,
                                    jnp.where(other == gate, jnp.where(row_id > n2, 1, 0), 0))
        ch_ref[j] = jnp.where(row_id == 0, 1.0, jnp.where(past, jnp.where(rank < MOBA_TOPK, 1.0, 0.0), 0.0))

    def stage(k, ref):
        n0 = pl.multiple_of(jnp.clip(k - 1, 0, n_blocks - 1) * blk, blk)
        for j in heads:
            ref[j] = jnp.dot(kr_ref[j, pl.ds(n0, blk), :], qs_t[j], preferred_element_type=F32)

    def consume(k, ref, carry):
        n0 = pl.multiple_of(jnp.where(k == 0, qb, k - 1) * blk, blk)
        out = []
        for j in heads:
            m, acc = carry[j]
            taken = ch_ref[j, pl.ds(k, 1), :] > 0.5
            m_new = jnp.maximum(m, jnp.where(taken, jnp.max(ref[j], axis=0, keepdims=True), NEG_INF))
            alpha = jnp.exp2(m - m_new)
            p = jnp.exp2(ref[j] - jnp.where(taken, m_new, -NEG_INF)).astype(BF16)
            out.append((m_new, alpha * acc + jnp.dot(vt_ref[j, :, pl.ds(n0, blk)], p,
                                                     preferred_element_type=F32)))
        return tuple(out)

    def two_blocks(i, carry):
        stage(2 * i + 1, sb_ref)
        carry = consume(2 * i, sa_ref, carry)
        stage(2 * i + 2, sa_ref)
        return consume(2 * i + 1, sb_ref, carry)

    init = tuple((jnp.full((1, blk), NEG_INF, F32), jnp.zeros((vt_ref.shape[1], blk), F32)) for _ in heads)
    final = lax.fori_loop(0, qb // 2 + 1, two_blocks, init)
    for j in heads:
        _, acc = final[j]
        cols = slice(j * hd, (j + 1) * hd)
        out_t = acc[0:hd, :] / acc[hd:hd + 1, :]
        o_ref[:, cols] = (out_t.T * _silu(g_ref[:, cols].astype(F32))).astype(o_ref.dtype)


def _moba(proj, cos_t, sin_t, *, batch, seq, heads, q_col, k_col, v_col, g_col, heads_per_step):
    nb = seq // MOBA_BLOCK
    hd = ATT_HEAD_DIM
    hw = hd * heads_per_step
    t = batch * seq
    assert heads % heads_per_step == 0 and all(c % hw == 0 for c in (q_col, k_col, v_col, g_col))
    qc, kc, vc, gc = (c // hw for c in (q_col, k_col, v_col, g_col))
    kern = functools.partial(_moba_kernel, n_blocks=nb, heads_per_step=heads_per_step)
    return pl.pallas_call(
        kern,
        grid=(batch, heads // heads_per_step, nb),
        in_specs=[
            pl.BlockSpec((MOBA_BLOCK, hw), lambda b, h, i: (b * nb + i, qc + h)),
            pl.BlockSpec((seq, hw), lambda b, h, i: (b, kc + h)),
            pl.BlockSpec((seq, hw), lambda b, h, i: (b, vc + h)),
            pl.BlockSpec((MOBA_BLOCK, hw), lambda b, h, i: (b * nb + i, gc + h)),
            _resident((seq, hd), lambda b, h, i: (0, 0)),
            _resident((seq, hd), lambda b, h, i: (0, 0)),
        ],
        out_specs=pl.BlockSpec((MOBA_BLOCK, hw), lambda b, h, i: (b * nb + i, h)),
        out_shape=jax.ShapeDtypeStruct((t, heads * hd), BF16),
        scratch_shapes=[
            pltpu.VMEM((heads_per_step, seq, hd), BF16),
            pltpu.VMEM((heads_per_step, hd + 16, seq), BF16),
            pltpu.VMEM((heads_per_step, nb + SUBLANES, hd), F32),
            pltpu.VMEM((heads_per_step, nb + SUBLANES, MOBA_BLOCK), F32),
            pltpu.VMEM((heads_per_step, MOBA_BLOCK, MOBA_BLOCK), F32),
            pltpu.VMEM((heads_per_step, MOBA_BLOCK, MOBA_BLOCK), F32),
        ],
        compiler_params=_cparams(("parallel", "parallel", "arbitrary")),
        name="moba0",
    )(proj, proj, proj, proj, cos_t, sin_t)


def _out0_kernel(ya_ref, yb_ref, x_ref, w_ref, g_ref, b_ref, o_ref, *, alpha):
    ka = ya_ref.shape[1]
    h = jnp.dot(ya_ref[...], w_ref[0:ka, :], preferred_element_type=F32)
    h = h + jnp.dot(yb_ref[...], w_ref[ka:, :], preferred_element_type=F32)
    v = alpha * x_ref[...] + h
    mu = jnp.mean(v, axis=1, keepdims=True)
    vc = v - mu
    var = jnp.mean(vc * vc, axis=1, keepdims=True)
    o_ref[...] = vc * lax.rsqrt(var + LN_EPS) * g_ref[0:1, :] + b_ref[0:1, :]


def _out0(ya, yb, x2, w, g, b, *, alpha, tm):
    t, d = x2.shape
    ka, kb = ya.shape[1], yb.shape[1]
    return pl.pallas_call(
        functools.partial(_out0_kernel, alpha=alpha),
        grid=(t // tm,),
        in_specs=[
            pl.BlockSpec((tm, ka), lambda i: (i, 0)),
            pl.BlockSpec((tm, kb), lambda i: (i, 0)),
            pl.BlockSpec((tm, d), lambda i: (i, 0)),
            _resident((ka + kb, d), lambda i: (0, 0)),
            _resident((SUBLANES, d), lambda i: (0, 0)),
            _resident((SUBLANES, d), lambda i: (0, 0)),
        ],
        out_specs=pl.BlockSpec((tm, d), lambda i: (i, 0)),
        out_shape=jax.ShapeDtypeStruct((t, d), F32),
        compiler_params=_cparams(("parallel",)),
        name="out0",
    )(ya, yb, x2, w, g, b)


def _inproj1_kernel(x_ref, w_ref, o_ref):
    o_ref[0] = jnp.dot(w_ref[...], x_ref[0].astype(BF16), preferred_element_type=F32).astype(o_ref.dtype)


def _inproj1(xt, wt):
    l, d, r = xt.shape
    n = wt.shape[0]
    return pl.pallas_call(
        _inproj1_kernel,
        grid=(l,),
        in_specs=[pl.BlockSpec((1, d, r), lambda i: (i, 0, 0)), _resident((n, d), lambda i: (0, 0))],
        out_specs=pl.BlockSpec((1, n, r), lambda i: (i, 0, 0)),
        out_shape=jax.ShapeDtypeStruct((l, n, r), BF16),
        compiler_params=_cparams(("parallel",)),
        name="inproj1",
    )(xt, wt)


def _cpow(lr, li, d, nbits):
    pr = jnp.ones(d.shape, F32)
    pi = jnp.zeros(d.shape, F32)
    br, bi = lr, li
    for bit in range(nbits):
        on = ((d >> bit) & 1) == 1
        nr = pr * br - pi * bi
        ni = pr * bi + pi * br
        pr = jnp.where(on, nr, pr)
        pi = jnp.where(on, ni, pi)
        if bit + 1 < nbits:
            br, bi = br * br - bi * bi, 2.0 * br * bi
    return pr, pi


def _gelu_tanh(v):
    return 0.5 * v * (1.0 + jnp.tanh(math.sqrt(2.0 / math.pi) * (v + 0.044715 * (v * v * v))))


def _s5_group(j, u_ref, lamc_ref, lamr_ref, bb_ref, rep_ref, cc_ref, ca_ref, cbm_ref, dsk_ref, o_ref, tz_ref,
              cl_ref, *, chunk, chunks_per_seq):
    m = S5_GROUP
    n = S5_STATE
    lm = chunk * m
    r = u_ref.shape[2]
    u = u_ref[:, j * m:(j + 1) * m, :].reshape(lm, r)

    pos_per_blk = LANES // m
    n_lane_blk = lm // LANES
    lam_r, lam_i = lamc_ref[j, 0], lamc_ref[j, 1]
    lane = lax.broadcasted_iota(jnp.int32, (n, LANES), 1)
    pr, pi = _cpow(lam_r, lam_i, (pos_per_blk - 1) - lane // m, (pos_per_blk - 1).bit_length())
    lbr, lbi = lam_r, lam_i
    for _ in range(pos_per_blk.bit_length() - 1):
        lbr, lbi = lbr * lbr - lbi * lbi, 2.0 * lbr * lbi
    bbr = _dot_split(bb_ref[j, 0], rep_ref[...], 3)
    bbi = _dot_split(bb_ref[j, 1], rep_ref[...], 3)
    bl_r, bl_i = [None] * n_lane_blk, [None] * n_lane_blk
    for c in range(n_lane_blk - 1, -1, -1):
        bl_r[c] = pr * bbr - pi * bbi
        bl_i[c] = pr * bbi + pi * bbr
        pr, pi = pr * lbr - pi * lbi, pr * lbi + pi * lbr
    bl = jnp.concatenate([jnp.concatenate(bl_r, axis=1), jnp.concatenate(bl_i, axis=1)], axis=0)

    krev = _dot_hi(cc_ref[j], bl)

    col_s = lax.broadcasted_iota(jnp.int32, (m, lm), 1) // m
    for t in range(chunk):
        rolled = pltpu.roll(krev, ((t + 1) * m) % lm, 1)
        tz_ref[j, t * m:(t + 1) * m, :] = jnp.where(col_s <= t, rolled, 0.0).astype(BF16)

    y = jnp.dot(tz_ref[j], u, preferred_element_type=F32)

    e = jnp.dot(bl.astype(BF16), u, preferred_element_type=F32)
    lane_r = lax.broadcasted_iota(jnp.int32, (n, r), 1) % chunks_per_seq
    hr = jnp.where(lane_r >= 1, pltpu.roll(e[:n], 1, 1), 0.0)
    hi = jnp.where(lane_r >= 1, pltpu.roll(e[n:], 1, 1), 0.0)
    mr, mi = lbr, lbi
    for _ in range((chunk // pos_per_blk).bit_length() - 1):
        mr, mi = mr * mr - mi * mi, 2.0 * mr * mi
    mr = jnp.concatenate([mr] * (r // LANES), axis=1) if r > LANES else mr[:, :r]
    mi = jnp.concatenate([mi] * (r // LANES), axis=1) if r > LANES else mi[:, :r]
    step = 1
    while step < chunks_per_seq:
        sr = jnp.where(lane_r >= step, pltpu.roll(hr, step, 1), 0.0)
        si = jnp.where(lane_r >= step, pltpu.roll(hi, step, 1), 0.0)
        hr, hi = hr + mr * sr - mi * si, hi + mr * si + mi * sr
        mr, mi = mr * mr - mi * mi, 2.0 * mr * mi
        step *= 2
    h_in = jnp.concatenate([hr, hi], axis=0).astype(BF16)

    lam_rr = jnp.broadcast_to(lamr_ref[j, 0:1, :], (m, 2 * n))
    lam_ri = jnp.broadcast_to(lamr_ref[j, 1:2, :], (m, 2 * n))
    ca, cbm = ca_ref[j], cbm_ref[j]
    qr, qi = lam_rr, lam_ri
    for t in range(chunk):
        cl_ref[j, t * m:(t + 1) * m, :] = (ca * qr + cbm * qi).astype(BF16)
        if t + 1 < chunk:
            qr, qi = qr * lam_rr - qi * lam_ri, qr * lam_ri + qi * lam_rr
    y = y + jnp.dot(cl_ref[j], h_in, preferred_element_type=F32)

    dsk = jnp.concatenate([dsk_ref[j]] * chunk, axis=0)
    dsk = jnp.concatenate([dsk] * (r // LANES), axis=1) if r > LANES else dsk[:, :r]
    y = y + dsk * u.astype(F32)
    o_ref[:, j * m:(j + 1) * m, :] = _gelu_tanh(y).astype(o_ref.dtype).reshape(chunk, m, r)


def _s5_kernel(*refs, chunk, chunks_per_seq, groups_per_step):
    for j in range(groups_per_step):
        _s5_group(j, *refs, chunk=chunk, chunks_per_seq=chunks_per_seq)


def _s5(ug, lamc, lamr, bb, rep, cc, ca, cbm, dsk, *, chunks_per_seq):
    chunk, _, r = ug.shape
    g = lamc.shape[0]
    m, n = S5_GROUP, S5_STATE
    lm = chunk * m
    gps = 2
    kern = functools.partial(_s5_kernel, chunk=chunk, chunks_per_seq=chunks_per_seq, groups_per_step=gps)
    p4 = lambda i: (i, 0, 0, 0)
    p3 = lambda i: (i, 0, 0)
    return pl.pallas_call(
        kern,
        grid=(g // gps,),
        in_specs=[
            pl.BlockSpec((chunk, gps * m, r), lambda i: (0, i, 0)),
            pl.BlockSpec((gps, 2, n, LANES), p4),
            pl.BlockSpec((gps, SUBLANES, 2 * n), p3),
            pl.BlockSpec((gps, 2, n, m), p4),
            _resident((m, LANES), lambda i: (0, 0)),
            pl.BlockSpec((gps, m, 2 * n), p3),
            pl.BlockSpec((gps, m, 2 * n), p3),
            pl.BlockSpec((gps, m, 2 * n), p3),
            pl.BlockSpec((gps, m, LANES), p3),
        ],
        out_specs=pl.BlockSpec((chunk, gps * m, r), lambda i: (0, i, 0)),
        out_shape=jax.ShapeDtypeStruct((chunk, g * m, r), BF16),
        scratch_shapes=[pltpu.VMEM((gps, lm, lm), BF16), pltpu.VMEM((gps, lm, 2 * n), BF16)],
        compiler_params=_cparams(("parallel",)),
        name="s5scan",
    )(ug, lamc, lamr, bb, rep, cc, ca, cbm, dsk)


def _out1_kernel(y_ref, gate_ref, x_ref, wg_ref, wo_ref, g_ref, b_ref, o_ref, v_ref, s_ref, *, alpha):
    w = y_ref.shape[1]
    d = x_ref.shape[1]
    mc = 512
    y = y_ref[0]
    for c0 in range(0, w, mc):
        ga = jnp.dot(wg_ref[c0:c0 + mc, :], y, preferred_element_type=F32)
        gb = jnp.dot(wg_ref[w + c0:w + c0 + mc, :], y, preferred_element_type=F32)
        v_ref[c0:c0 + mc, :] = (ga * _sigmoid(gb) * _silu(gate_ref[0, c0:c0 + mc, :].astype(F32))).astype(BF16)
    v = v_ref[...]
    for c0 in range(0, d, mc):
        s_ref[c0:c0 + mc, :] = alpha * x_ref[0, c0:c0 + mc, :] + jnp.dot(
            wo_ref[c0:c0 + mc, :], v, preferred_element_type=F32)
    s = s_ref[...]
    mu = jnp.mean(s, axis=0, keepdims=True)
    sc = s - mu
    var = jnp.mean(sc * sc, axis=0, keepdims=True)
    o_ref[0] = sc * lax.rsqrt(var + LN_EPS) * g_ref[...] + b_ref[...]


def _out1(y3, ug, xt, wgt, wot, g_col, b_col, *, alpha):
    l, w, r = y3.shape
    d = xt.shape[1]
    gate_blk = 1
    return pl.pallas_call(
        functools.partial(_out1_kernel, alpha=alpha),
        grid=(l,),
        in_specs=[
            pl.BlockSpec((1, w, r), lambda i: (i, 0, 0)),
            pl.BlockSpec((1, w, r), lambda i: (i, gate_blk, 0)),
            pl.BlockSpec((1, d, r), lambda i: (i, 0, 0)),
            _resident((2 * w, w), lambda i: (0, 0)),
            _resident((d, w), lambda i: (0, 0)),
            _resident((d, 1), lambda i: (0, 0)),
            _resident((d, 1), lambda i: (0, 0)),
        ],
        out_specs=pl.BlockSpec((1, d, r), lambda i: (i, 0, 0)),
        out_shape=jax.ShapeDtypeStruct((l, d, r), F32),
        scratch_shapes=[pltpu.VMEM((w, r), BF16), pltpu.VMEM((d, r), F32)],
        compiler_params=_cparams(("parallel",)),
        name="out1",
    )(y3, ug, xt, wgt, wot, g_col, b_col)


def _pad_rows(v, rows=SUBLANES):
    v = jnp.atleast_2d(v.astype(F32))
    return jnp.pad(v, ((0, rows - v.shape[0]), (0, 0)))


def _pad_lanes(v, lanes=LANES):
    return jnp.pad(v, [(0, 0)] * (v.ndim - 1) + [(0, lanes - v.shape[-1])])


def _layer0(x2, batch, seq, in_w, conv_w, conv_b, dt_bias, a_log, d_skip, norm_g, out_w, ln_g, ln_b, alpha):
    d = x2.shape[1]
    ssd_heads = dt_bias.shape[0]
    width = ssd_heads * SSD_HEAD_DIM
    conv_dim = conv_w.shape[1]
    d_state = (conv_dim - width) // (2 * SSD_GROUPS)
    att_width = (in_w.shape[1] - width - conv_dim - ssd_heads) // 4
    att_heads = att_width // ATT_HEAD_DIM

    z_end = width
    xbc_end = z_end + conv_dim
    dt_end = xbc_end + ssd_heads
    w_all = in_w.astype(BF16)
    w_b = w_all[:, dt_end:]
    w_dt = _pad_lanes(w_all[:, xbc_end:dt_end])
    z_col, xbc_col = 0, z_end
    q_col = xbc_end
    k_col, v_col, g_col = q_col + att_width, q_col + 2 * att_width, q_col + 3 * att_width

    proj, dt_raw = _inproj(x2, w_all, xbc_end, w_b, w_dt, tm=1024, tn=1024)

    expand = (jnp.arange(LANES)[:, None] == (jnp.arange(width)[None, :] // SSD_HEAD_DIM)).astype(BF16)
    ya = _ssd(proj, dt_raw, _pad_rows(conv_w), _pad_rows(conv_b), _pad_rows(_pad_lanes(dt_bias.astype(F32))),
              _pad_rows(_pad_lanes(-jnp.exp(a_log.astype(F32)))),
              _pad_rows(jnp.repeat(d_skip.astype(F32), SSD_HEAD_DIM)), _pad_rows(norm_g), expand,
              batch=batch, seq=seq, width=width, n_groups=SSD_GROUPS, d_state=d_state,
              z_col=z_col, xbc_col=xbc_col)

    half = ROPE_DIM // 2
    inv_freq = ROPE_THETA ** (-(jnp.arange(half, dtype=F32) * 2.0 / ROPE_DIM))
    ang = jnp.arange(seq, dtype=F32)[:, None] * inv_freq[None, :]
    ones = jnp.ones((seq, ATT_HEAD_DIM - ROPE_DIM), F32)
    cos_t = jnp.concatenate([jnp.cos(ang), jnp.cos(ang), ones], axis=1)
    sin_t = jnp.concatenate([-jnp.sin(ang), jnp.sin(ang), 0.0 * ones], axis=1)
    yb = _moba(proj, cos_t, sin_t, batch=batch, seq=seq, heads=att_heads,
               q_col=q_col, k_col=k_col, v_col=v_col, g_col=g_col, heads_per_step=4)

    return _out0(ya, yb, x2, out_w.astype(BF16), _pad_rows(ln_g), _pad_rows(ln_b), alpha=alpha, tm=512)


def _layer1(x2, batch, seq, in_w, lam_re, lam_im, log_dt, b_re, b_im, c_re, c_im, d_skip, glu_w, out_w,
            ln_g, ln_b, alpha):
    t, d = x2.shape
    chunk = S5_L
    cps = seq // chunk
    r = batch * cps
    groups, n = lam_re.shape
    m = S5_GROUP
    w = groups * m

    lre, lim = lam_re.astype(F32), lam_im.astype(F32)
    dt = jnp.exp(log_dt.astype(F32))[:, None]
    mag = jnp.exp(lre * dt)
    lbr, lbi = mag * jnp.cos(lim * dt), mag * jnp.sin(lim * dt)
    den = lre * lre + lim * lim
    fr = ((lbr - 1.0) * lre + lbi * lim) / den
    fi = (lbi * lre - (lbr - 1.0) * lim) / den
    bre, bim = b_re.astype(F32), b_im.astype(F32)
    bbr = fr[..., None] * bre - fi[..., None] * bim
    bbi = fr[..., None] * bim + fi[..., None] * bre
    lamc = jnp.broadcast_to(jnp.stack([lbr, lbi], axis=1)[..., None], (groups, 2, n, LANES))
    lamr = jnp.stack([jnp.concatenate([lbr, lbr], -1), jnp.concatenate([lbi, lbi], -1)], axis=1)
    lamr = jnp.pad(lamr, ((0, 0), (0, SUBLANES - 2), (0, 0)))
    bb = jnp.stack([bbr, bbi], axis=1)
    rep = (jnp.arange(m)[:, None] == (jnp.arange(LANES)[None, :] % m)).astype(BF16)
    cr, ci = c_re.astype(F32), c_im.astype(F32)
    cc = jnp.concatenate([cr, -ci], axis=-1)
    cbm = jnp.concatenate([-ci, -cr], axis=-1)
    dsk = jnp.broadcast_to(d_skip.astype(F32).reshape(groups, m, 1), (groups, m, LANES))

    xt = x2.reshape(r, chunk, d).transpose(1, 2, 0)
    ug = _inproj1(xt, in_w.T.astype(BF16))
    y3 = _s5(ug, lamc, lamr, bb, rep, cc, cc, cbm, dsk, chunks_per_seq=cps)
    o3 = _out1(y3, ug, xt, glu_w.T.astype(BF16), out_w.T.astype(BF16),
               ln_g.astype(F32)[:, None], ln_b.astype(F32)[:, None], alpha=alpha)
    return o3.transpose(2, 0, 1).reshape(t, d)


def kernel(x, in0_w, conv_w, conv_b, dt_bias, a_log, ssd_d, ssd_norm_g, out0_w, in1_w, s5_lam_re, s5_lam_im,
           s5_log_dt, s5_b_re, s5_b_im, s5_c_re, s5_c_im, s5_d, glu_w, out1_w, ln_g, ln_b):
    batch, seq, d = x.shape
    depth = ln_g.shape[0]
    alpha = (2 * depth) ** 0.25
    x2 = x.reshape(batch * seq, d)
    for layer in range(depth):
        i = layer // 2
        if layer % 2 == 0:
            x2 = _layer0(x2, batch, seq, in0_w[i], conv_w[i], conv_b[i], dt_bias[i], a_log[i], ssd_d[i],
                         ssd_norm_g[i], out0_w[i], ln_g[layer], ln_b[layer], alpha)
        else:
            x2 = _layer1(x2, batch, seq, in1_w[i], s5_lam_re[i], s5_lam_im[i], s5_log_dt[i], s5_b_re[i],
                         s5_b_im[i], s5_c_re[i], s5_c_im[i], s5_d[i], glu_w[i], out1_w[i],
                         ln_g[layer], ln_b[layer], alpha)
    return x2.reshape(batch, seq, d).astype(x.dtype)
```

```python
import functools
import math

import jax
import jax.numpy as jnp
from jax import lax
from jax.experimental import pallas as pl
from jax.experimental.pallas import tpu as pltpu

F32 = jnp.float32
BF16 = jnp.bfloat16

SSD_HEAD_DIM = 64
SSD_GROUPS = 4
SSD_STATE = 128
SSD_CONV = 4
SSD_CHUNK = 256
ATT_HEAD_DIM = 128
MOBA_BLOCK = 256
MOBA_TOPK = 3
ROPE_THETA = 500000.0
ROPE_DIM = ATT_HEAD_DIM // 4
S5_GROUP = 16
S5_STATE = 64
LN_EPS = 1e-5
RMS_EPS = 1e-5
NEG_INF = -1e30

LANES = 128
SUBLANES = 8
VMEM_LIMIT = 56 * 1024 * 1024

S5_L = 64


def _cparams(sem):
    return pltpu.CompilerParams(dimension_semantics=sem, vmem_limit_bytes=VMEM_LIMIT)


def _resident(shape, index_map):
    return pl.BlockSpec(shape, index_map, pipeline_mode=pl.Buffered(1))


def _sigmoid(v):
    return 0.5 + 0.5 * jnp.tanh(0.5 * v)


def _silu(v):
    h = 0.5 * v
    return h + h * jnp.tanh(h)


def _nt_dot(a, b):
    return lax.dot_general(a, b, (((1,), (1,)), ((), ())), preferred_element_type=F32)


def _tn_dot(a, b):
    return lax.dot_general(a, b, (((0,), (0,)), ((), ())), preferred_element_type=F32)


def _dot_hi(a, b):
    a_hi = a.astype(BF16)
    b_hi = b.astype(BF16)
    a_lo = (a - a_hi.astype(F32)).astype(BF16)
    b_lo = (b - b_hi.astype(F32)).astype(BF16)
    return (jnp.dot(a_hi, b_hi, preferred_element_type=F32) + jnp.dot(a_hi, b_lo, preferred_element_type=F32)
            + jnp.dot(a_lo, b_hi, preferred_element_type=F32))


def _dot_split(a, b, passes, split_lhs=True):
    rem = a if split_lhs else b
    acc = None
    for _ in range(passes):
        piece = rem.astype(BF16)
        part = jnp.dot(piece, b, preferred_element_type=F32) if split_lhs else jnp.dot(
            a, piece, preferred_element_type=F32)
        acc = part if acc is None else acc + part
        rem = rem - piece.astype(F32)
    return acc


def _inproj_kernel(x_ref, wa_ref, wb_ref, wdt_ref, o_ref, dt_ref, xb_ref, *, tiles_a):
    j = pl.program_id(1)

    @pl.when(j == 0)
    def _():
        xb = x_ref[...].astype(BF16)
        xb_ref[...] = xb
        dt_ref[...] = jnp.dot(xb, wdt_ref[...], preferred_element_type=F32)

    @pl.when(j < tiles_a)
    def _():
        o_ref[...] = jnp.dot(xb_ref[...], wa_ref[...], preferred_element_type=F32).astype(o_ref.dtype)

    @pl.when(j >= tiles_a)
    def _():
        o_ref[...] = jnp.dot(xb_ref[...], wb_ref[...], preferred_element_type=F32).astype(o_ref.dtype)


def _inproj(x2, w_a, cols_a, w_b, w_dt, tm, tn):
    t, d = x2.shape
    assert cols_a % tn == 0 and w_b.shape[1] % tn == 0
    tiles_a, tiles_b = cols_a // tn, w_b.shape[1] // tn
    n = (tiles_a + tiles_b) * tn
    return pl.pallas_call(
        functools.partial(_inproj_kernel, tiles_a=tiles_a),
        grid=(t // tm, tiles_a + tiles_b),
        in_specs=[
            pl.BlockSpec((tm, d), lambda i, j: (i, 0)),
            pl.BlockSpec((d, tn), lambda i, j: (0, jnp.minimum(j, tiles_a - 1))),
            pl.BlockSpec((d, tn), lambda i, j: (0, jnp.maximum(j - tiles_a, 0))),
            _resident((d, LANES), lambda i, j: (0, 0)),
        ],
        out_specs=[
            pl.BlockSpec((tm, tn), lambda i, j: (i, j)),
            pl.BlockSpec((tm, LANES), lambda i, j: (i, 0)),
        ],
        out_shape=[jax.ShapeDtypeStruct((t, n), BF16), jax.ShapeDtypeStruct((t, LANES), F32)],
        scratch_shapes=[pltpu.VMEM((tm, d), BF16)],
        compiler_params=_cparams(("parallel", "arbitrary")),
        name="inproj0",
    )(x2, w_a, w_b, w_dt)


def _ssd_kernel(z_ref, xs_ref, bm_ref, cm_ref, dt_ref, cw_ref, cb_ref, dtb_ref, aneg_ref, dsk_ref, ng_ref,
                exp_ref, o_ref, xpad_ref, state_ref, act_ref, y_ref, *, n_groups, head_dim):
    c = pl.program_id(1)
    chunk, width = z_ref.shape
    d_state = bm_ref.shape[1] // n_groups
    heads_per_group = width // head_dim // n_groups
    gw = heads_per_group * head_dim

    @pl.when(c == 0)
    def _():
        xpad_ref[0:SUBLANES, :] = jnp.zeros((SUBLANES, xpad_ref.shape[1]), F32)
        state_ref[...] = jnp.zeros_like(state_ref)

    strip = 512
    col0 = 0
    for src in (xs_ref, bm_ref, cm_ref):
        for s0 in range(0, src.shape[1], strip):
            cols = slice(col0 + s0, col0 + s0 + strip)
            xs_ = src[:, s0:s0 + strip].astype(F32)
            xpad_ref[SUBLANES:, cols] = xs_
            acc = cb_ref[0:1, cols] + cw_ref[SSD_CONV - 1:SSD_CONV, cols] * xs_
            for k in range(1, SSD_CONV):
                acc = acc + cw_ref[SSD_CONV - 1 - k:SSD_CONV - k, cols] * xpad_ref[SUBLANES - k:SUBLANES - k + chunk, cols]
            act_ref[:, cols] = _silu(acc)
            xpad_ref[0:SUBLANES, cols] = xs_[chunk - SUBLANES:, :]
        col0 += src.shape[1]

    dt_in = dt_ref[...] + dtb_ref[0:1, :]
    dt = jnp.maximum(dt_in, 0.0) + jnp.log(1.0 + jnp.exp(-jnp.abs(dt_in)))
    a = dt * aneg_ref[0:1, :]
    row = lax.broadcasted_iota(jnp.int32, (chunk, chunk), 0)
    col = lax.broadcasted_iota(jnp.int32, (chunk, chunk), 1)
    causal = row >= col
    cs = _dot_split(jnp.where(causal, 1.0, 0.0).astype(BF16), a, 3, split_lhs=False)
    cs2 = cs * math.log2(math.e)
    cs2_t = cs2.T
    cs_last = cs[chunk - 1:chunk, :]
    expand = exp_ref[...]
    dt_x = _dot_split(dt, expand, 2)
    ecs_x = _dot_split(jnp.exp(cs), expand, 2)
    dec_x = _dot_split(jnp.exp(cs_last - cs), expand, 2)
    sdec_x = _dot_split(jnp.broadcast_to(jnp.exp(cs_last), (SUBLANES, LANES)), expand, 2)[0:1, :]

    lane = lax.broadcasted_iota(jnp.int32, (chunk, LANES), 1)
    first_half = lane < head_dim
    for g in range(n_groups):
        gs = slice(g * gw, (g + 1) * gw)
        b_g = act_ref[:, width + g * d_state:width + (g + 1) * d_state].astype(BF16)
        c_g = act_ref[:, width + (n_groups + g) * d_state:width + (n_groups + g + 1) * d_state].astype(BF16)
        xs_g = act_ref[:, gs]
        xdt = xs_g * dt_x[:, gs]
        xdt_b = xdt.astype(BF16)
        cb = jnp.where(causal, _nt_dot(c_g, b_g), 0.0)
        st = state_ref[:, gs]
        y_g = jnp.dot(c_g, st.astype(BF16), preferred_element_type=F32) * ecs_x[:, gs] + dsk_ref[0:1, gs] * xs_g
        state_ref[:, gs] = st * sdec_x[:, gs] + _tn_dot(b_g, (xdt * dec_x[:, gs]).astype(BF16))
        for pr in range(heads_per_group // 2):
            xp = xdt_b[:, pr * LANES:(pr + 1) * LANES]
            halves = []
            for half in range(2):
                h = g * heads_per_group + 2 * pr + half
                seg = cs2[:, h:h + 1] - cs2_t[h:h + 1, :]
                m = (cb * jnp.exp2(jnp.minimum(seg, 0.0))).astype(BF16)
                halves.append(jnp.dot(m, xp, preferred_element_type=F32))
            y_pair = jnp.where(first_half, halves[0], halves[1])
            y_ref[:, g * gw + pr * LANES:g * gw + (pr + 1) * LANES] = y_pair + y_g[:, pr * LANES:(pr + 1) * LANES]

    yz = y_ref[...] * _silu(z_ref[...].astype(F32))
    ms = jnp.mean(yz * yz, axis=1, keepdims=True)
    o_ref[...] = (yz * lax.rsqrt(ms + RMS_EPS) * ng_ref[0:1, :]).astype(o_ref.dtype)


def _ssd(proj, dt_raw, cw, cb, dtb, aneg, dsk, ng, expand, *, batch, seq, width, n_groups, d_state,
         z_col, xbc_col):
    conv_dim = width + 2 * n_groups * d_state
    bc_w = n_groups * d_state
    nc = seq // SSD_CHUNK
    t = batch * seq
    kern = functools.partial(_ssd_kernel, n_groups=n_groups, head_dim=SSD_HEAD_DIM)
    b_col, c_col = xbc_col + width, xbc_col + width + bc_w
    assert z_col % width == 0 and xbc_col % width == 0 and b_col % bc_w == 0 and c_col % bc_w == 0
    const = lambda b, c: (0, 0)
    return pl.pallas_call(
        kern,
        grid=(batch, nc),
        in_specs=[
            pl.BlockSpec((SSD_CHUNK, width), lambda b, c: (b * nc + c, z_col // width)),
            pl.BlockSpec((SSD_CHUNK, width), lambda b, c: (b * nc + c, xbc_col // width)),
            pl.BlockSpec((SSD_CHUNK, bc_w), lambda b, c: (b * nc + c, b_col // bc_w)),
            pl.BlockSpec((SSD_CHUNK, bc_w), lambda b, c: (b * nc + c, c_col // bc_w)),
            pl.BlockSpec((SSD_CHUNK, LANES), lambda b, c: (b * nc + c, 0)),
            _resident((SUBLANES, conv_dim), const),
            _resident((SUBLANES, conv_dim), const),
            _resident((SUBLANES, LANES), const),
            _resident((SUBLANES, LANES), const),
            _resident((SUBLANES, width), const),
            _resident((SUBLANES, width), const),
            _resident((LANES, width), const),
        ],
        out_specs=pl.BlockSpec((SSD_CHUNK, width), lambda b, c: (b * nc + c, 0)),
        out_shape=jax.ShapeDtypeStruct((t, width), BF16),
        scratch_shapes=[
            pltpu.VMEM((SUBLANES + SSD_CHUNK, conv_dim), F32),
            pltpu.VMEM((d_state, width), F32),
            pltpu.VMEM((SSD_CHUNK, conv_dim), F32),
            pltpu.VMEM((SSD_CHUNK, width), F32),
        ],
        compiler_params=_cparams(("parallel", "arbitrary")),
        name="ssd0",
    )(proj, proj, proj, proj, dt_raw, cw, cb, dtb, aneg, dsk, ng, expand)


def _rope(v, cos, sin):
    half = ROPE_DIM // 2
    lane = lax.broadcasted_iota(jnp.int32, v.shape, 1)
    partner = jnp.where(lane < half, pltpu.roll(v, LANES - half, 1), pltpu.roll(v, half, 1))
    return v * cos + partner * sin


def _moba_kernel(q_ref, k_ref, v_ref, g_ref, cos_ref, sin_ref, o_ref, kr_ref, vt_ref, km_ref, ch_ref,
                 sa_ref, sb_ref, *, n_blocks, heads_per_step):
    qb = pl.program_id(2)
    blk = MOBA_BLOCK
    hd = ATT_HEAD_DIM
    scale = hd ** -0.5 * math.log2(math.e)
    heads = range(heads_per_step)

    n_rows = km_ref.shape[1]

    @pl.when(qb == 0)
    def _():
        km_ref[...] = jnp.zeros_like(km_ref)
        for n in range(n_blocks):
            rows = slice(n * blk, (n + 1) * blk)
            for j in heads:
                cols = slice(j * hd, (j + 1) * hd)
                kr = _rope(k_ref[rows, cols].astype(F32), cos_ref[rows, :], sin_ref[rows, :])
                kr_ref[j, rows, :] = kr.astype(BF16)
                if n + 1 < n_rows:
                    km_ref[j, n + 1:n + 2, :] = jnp.mean(kr, axis=0, keepdims=True)
                vt_ref[j, 0:hd, rows] = v_ref[rows, cols].astype(F32).T.astype(BF16)
                vt_ref[j, hd:, rows] = jnp.ones((vt_ref.shape[1] - hd, blk), BF16)

    q0 = pl.multiple_of(qb * blk, blk)
    cos_q = cos_ref[pl.ds(q0, blk), :]
    sin_q = sin_ref[pl.ds(q0, blk), :]
    row_id = lax.broadcasted_iota(jnp.int32, (n_rows, blk), 0)
    past = (row_id >= 1) & (row_id <= qb)
    key_i = lax.broadcasted_iota(jnp.int32, (blk, blk), 0)
    qry_i = lax.broadcasted_iota(jnp.int32, (blk, blk), 1)

    qs_t = []
    for j in heads:
        qr_t = _rope(q_ref[:, j * hd:(j + 1) * hd].astype(F32), cos_q, sin_q).T
        qs_t.append((qr_t * scale).astype(BF16))

        s_own = jnp.dot(kr_ref[j, pl.ds(q0, blk), :], qs_t[j], preferred_element_type=F32)
        sa_ref[j] = jnp.where(key_i <= qry_i, s_own, NEG_INF)

        gate = jnp.where(past, _dot_hi(km_ref[j], qr_t), NEG_INF)
        rank = jnp.zeros((n_rows, blk), jnp.int32)
        for n2 in range(1, n_blocks):
            other = gate[n2:n2 + 1, :]
            rank = rank + jnp.where(other > gate, 1,
                                    jnp.where(other == gate, jnp.where(row_id > n2, 1, 0), 0))
        ch_ref[j] = jnp.where(row_id == 0, 1.0, jnp.where(past, jnp.where(rank < MOBA_TOPK, 1.0, 0.0), 0.0))

    def stage(k, ref):
        n0 = pl.multiple_of(jnp.clip(k - 1, 0, n_blocks - 1) * blk, blk)
        for j in heads:
            ref[j] = jnp.dot(kr_ref[j, pl.ds(n0, blk), :], qs_t[j], preferred_element_type=F32)

    def consume(k, ref, carry):
        n0 = pl.multiple_of(jnp.where(k == 0, qb, k - 1) * blk, blk)
        out = []
        for j in heads:
            m, acc = carry[j]
            taken = ch_ref[j, pl.ds(k, 1), :] > 0.5
            m_new = jnp.maximum(m, jnp.where(taken, jnp.max(ref[j], axis=0, keepdims=True), NEG_INF))
            alpha = jnp.exp2(m - m_new)
            p = jnp.exp2(ref[j] - jnp.where(taken, m_new, -NEG_INF)).astype(BF16)
            out.append((m_new, alpha * acc + jnp.dot(vt_ref[j, :, pl.ds(n0, blk)], p,
                                                     preferred_element_type=F32)))
        return tuple(out)

    def two_blocks(i, carry):
        stage(2 * i + 1, sb_ref)
        carry = consume(2 * i, sa_ref, carry)
        stage(2 * i + 2, sa_ref)
        return consume(2 * i + 1, sb_ref, carry)

    init = tuple((jnp.full((1, blk), NEG_INF, F32), jnp.zeros((vt_ref.shape[1], blk), F32)) for _ in heads)
    final = lax.fori_loop(0, qb // 2 + 1, two_blocks, init)
    for j in heads:
        _, acc = final[j]
        cols = slice(j * hd, (j + 1) * hd)
        out_t = acc[0:hd, :] / acc[hd:hd + 1, :]
        o_ref[:, cols] = (out_t.T * _silu(g_ref[:, cols].astype(F32))).astype(o_ref.dtype)


def _moba(proj, cos_t, sin_t, *, batch, seq, heads, q_col, k_col, v_col, g_col, heads_per_step):
    nb = seq // MOBA_BLOCK
    hd = ATT_HEAD_DIM
    hw = hd * heads_per_step
    t = batch * seq
    assert heads % heads_per_step == 0 and all(c % hw == 0 for c in (q_col, k_col, v_col, g_col))
    qc, kc, vc, gc = (c // hw for c in (q_col, k_col, v_col, g_col))
    kern = functools.partial(_moba_kernel, n_blocks=nb, heads_per_step=heads_per_step)
    return pl.pallas_call(
        kern,
        grid=(batch, heads // heads_per_step, nb),
        in_specs=[
            pl.BlockSpec((MOBA_BLOCK, hw), lambda b, h, i: (b * nb + i, qc + h)),
            pl.BlockSpec((seq, hw), lambda b, h, i: (b, kc + h)),
            pl.BlockSpec((seq, hw), lambda b, h, i: (b, vc + h)),
            pl.BlockSpec((MOBA_BLOCK, hw), lambda b, h, i: (b * nb + i, gc + h)),
            _resident((seq, hd), lambda b, h, i: (0, 0)),
            _resident((seq, hd), lambda b, h, i: (0, 0)),
        ],
        out_specs=pl.BlockSpec((MOBA_BLOCK, hw), lambda b, h, i: (b * nb + i, h)),
        out_shape=jax.ShapeDtypeStruct((t, heads * hd), BF16),
        scratch_shapes=[
            pltpu.VMEM((heads_per_step, seq, hd), BF16),
            pltpu.VMEM((heads_per_step, hd + 16, seq), BF16),
            pltpu.VMEM((heads_per_step, nb, hd), F32),
            pltpu.VMEM((heads_per_step, nb, MOBA_BLOCK), F32),
            pltpu.VMEM((heads_per_step, MOBA_BLOCK, MOBA_BLOCK), F32),
            pltpu.VMEM((heads_per_step, MOBA_BLOCK, MOBA_BLOCK), F32),
        ],
        compiler_params=_cparams(("parallel", "parallel", "arbitrary")),
        name="moba0",
    )(proj, proj, proj, proj, cos_t, sin_t)


def _out0_kernel(ya_ref, yb_ref, x_ref, w_ref, g_ref, b_ref, o_ref, *, alpha):
    ka = ya_ref.shape[1]
    h = jnp.dot(ya_ref[...], w_ref[0:ka, :], preferred_element_type=F32)
    h = h + jnp.dot(yb_ref[...], w_ref[ka:, :], preferred_element_type=F32)
    v = alpha * x_ref[...] + h
    mu = jnp.mean(v, axis=1, keepdims=True)
    vc = v - mu
    var = jnp.mean(vc * vc, axis=1, keepdims=True)
    o_ref[...] = vc * lax.rsqrt(var + LN_EPS) * g_ref[0:1, :] + b_ref[0:1, :]


def _out0(ya, yb, x2, w, g, b, *, alpha, tm):
    t, d = x2.shape
    ka, kb = ya.shape[1], yb.shape[1]
    return pl.pallas_call(
        functools.partial(_out0_kernel, alpha=alpha),
        grid=(t // tm,),
        in_specs=[
            pl.BlockSpec((tm, ka), lambda i: (i, 0)),
            pl.BlockSpec((tm, kb), lambda i: (i, 0)),
            pl.BlockSpec((tm, d), lambda i: (i, 0)),
            _resident((ka + kb, d), lambda i: (0, 0)),
            _resident((SUBLANES, d), lambda i: (0, 0)),
            _resident((SUBLANES, d), lambda i: (0, 0)),
        ],
        out_specs=pl.BlockSpec((tm, d), lambda i: (i, 0)),
        out_shape=jax.ShapeDtypeStruct((t, d), F32),
        compiler_params=_cparams(("parallel",)),
        name="out0",
    )(ya, yb, x2, w, g, b)


def _inproj1_kernel(x_ref, w_ref, o_ref):
    o_ref[0] = jnp.dot(w_ref[...], x_ref[0].astype(BF16), preferred_element_type=F32).astype(o_ref.dtype)


def _inproj1(xt, wt):
    l, d, r = xt.shape
    n = wt.shape[0]
    return pl.pallas_call(
        _inproj1_kernel,
        grid=(l,),
        in_specs=[pl.BlockSpec((1, d, r), lambda i: (i, 0, 0)), _resident((n, d), lambda i: (0, 0))],
        out_specs=pl.BlockSpec((1, n, r), lambda i: (i, 0, 0)),
        out_shape=jax.ShapeDtypeStruct((l, n, r), BF16),
        compiler_params=_cparams(("parallel",)),
        name="inproj1",
    )(xt, wt)


def _cpow(lr, li, d, nbits):
    pr = jnp.ones(d.shape, F32)
    pi = jnp.zeros(d.shape, F32)
    br, bi = lr, li
    for bit in range(nbits):
        on = ((d >> bit) & 1) == 1
        nr = pr * br - pi * bi
        ni = pr * bi + pi * br
        pr = jnp.where(on, nr, pr)
        pi = jnp.where(on, ni, pi)
        if bit + 1 < nbits:
            br, bi = br * br - bi * bi, 2.0 * br * bi
    return pr, pi


def _gelu_tanh(v):
    c = math.sqrt(2.0 / math.pi)
    h = 0.5 * v
    return h + h * jnp.tanh(v * (c + (c * 0.044715) * (v * v)))


def _s5_group(j, u_ref, lamc_ref, lamr_ref, bb_ref, rep_ref, cc_ref, ca_ref, cbm_ref, dsk_ref, o_ref, tz_ref,
              *, chunk, chunks_per_seq):
    m = S5_GROUP
    n = S5_STATE
    lm = chunk * m
    r = u_ref.shape[2]
    u = u_ref[:, j * m:(j + 1) * m, :].reshape(lm, r)

    pos_per_blk = LANES // m
    n_lane_blk = lm // LANES
    lam_r, lam_i = lamc_ref[j, 0], lamc_ref[j, 1]
    lane = lax.broadcasted_iota(jnp.int32, (n, LANES), 1)
    pr, pi = _cpow(lam_r, lam_i, (pos_per_blk - 1) - lane // m, (pos_per_blk - 1).bit_length())
    lbr, lbi = lam_r, lam_i
    for _ in range(pos_per_blk.bit_length() - 1):
        lbr, lbi = lbr * lbr - lbi * lbi, 2.0 * lbr * lbi
    bbr = _dot_split(bb_ref[j, 0], rep_ref[...], 3)
    bbi = _dot_split(bb_ref[j, 1], rep_ref[...], 3)
    bl_r, bl_i = [None] * n_lane_blk, [None] * n_lane_blk
    for c in range(n_lane_blk - 1, -1, -1):
        bl_r[c] = pr * bbr - pi * bbi
        bl_i[c] = pr * bbi + pi * bbr
        pr, pi = pr * lbr - pi * lbi, pr * lbi + pi * lbr
    bl = jnp.concatenate([jnp.concatenate(bl_r, axis=1), jnp.concatenate(bl_i, axis=1)], axis=0)

    krev = _dot_hi(cc_ref[j], bl)
    col = lax.broadcasted_iota(jnp.int32, (m, lm), 1)
    row = lax.broadcasted_iota(jnp.int32, (m, lm), 0)
    d_lanes = jnp.concatenate([dsk_ref[j]] * n_lane_blk, axis=1)
    krev = krev + jnp.where(col == (chunk - 1) * m + row, d_lanes, 0.0)

    col_s = col // m
    for t in range(chunk):
        rolled = pltpu.roll(krev, ((t + 1) * m) % lm, 1)
        tz_ref[j, t * m:(t + 1) * m, 0:lm] = jnp.where(col_s <= t, rolled, 0.0).astype(BF16)

    e = jnp.dot(bl.astype(BF16), u, preferred_element_type=F32)
    lane_r = lax.broadcasted_iota(jnp.int32, (n, r), 1) % chunks_per_seq
    hr = jnp.where(lane_r >= 1, pltpu.roll(e[:n], 1, 1), 0.0)
    hi = jnp.where(lane_r >= 1, pltpu.roll(e[n:], 1, 1), 0.0)
    mr, mi = lbr, lbi
    for _ in range((chunk // pos_per_blk).bit_length() - 1):
        mr, mi = mr * mr - mi * mi, 2.0 * mr * mi
    mr = jnp.concatenate([mr] * (r // LANES), axis=1) if r > LANES else mr[:, :r]
    mi = jnp.concatenate([mi] * (r // LANES), axis=1) if r > LANES else mi[:, :r]
    step = 1
    while step < chunks_per_seq:
        sr = jnp.where(lane_r >= step, pltpu.roll(hr, step, 1), 0.0)
        si = jnp.where(lane_r >= step, pltpu.roll(hi, step, 1), 0.0)
        hr, hi = hr + mr * sr - mi * si, hi + mr * si + mi * sr
        mr, mi = mr * mr - mi * mi, 2.0 * mr * mi
        step *= 2
    h_in = jnp.concatenate([hr, hi], axis=0).astype(BF16)

    lam_rr = jnp.broadcast_to(lamr_ref[j, 0:1, :], (m, 2 * n))
    lam_ri = jnp.broadcast_to(lamr_ref[j, 1:2, :], (m, 2 * n))
    ca, cbm = ca_ref[j], cbm_ref[j]
    qr, qi = lam_rr, lam_ri
    for t in range(chunk):
        tz_ref[j, t * m:(t + 1) * m, lm:] = (ca * qr + cbm * qi).astype(BF16)
        if t + 1 < chunk:
            qr, qi = qr * lam_rr - qi * lam_ri, qr * lam_ri + qi * lam_rr

    y = jnp.dot(tz_ref[j], jnp.concatenate([u, h_in], axis=0), preferred_element_type=F32)
    o_ref[:, j * m:(j + 1) * m, :] = _gelu_tanh(y).astype(o_ref.dtype).reshape(chunk, m, r)


def _s5_kernel(*refs, chunk, chunks_per_seq, groups_per_step):
    for j in range(groups_per_step):
        _s5_group(j, *refs, chunk=chunk, chunks_per_seq=chunks_per_seq)


def _s5(ug, lamc, lamr, bb, rep, cc, ca, cbm, dsk, *, chunks_per_seq):
    chunk, _, r = ug.shape
    g = lamc.shape[0]
    m, n = S5_GROUP, S5_STATE
    lm = chunk * m
    gps = 2
    kern = functools.partial(_s5_kernel, chunk=chunk, chunks_per_seq=chunks_per_seq, groups_per_step=gps)
    p4 = lambda i: (i, 0, 0, 0)
    p3 = lambda i: (i, 0, 0)
    return pl.pallas_call(
        kern,
        grid=(g // gps,),
        in_specs=[
            pl.BlockSpec((chunk, gps * m, r), lambda i: (0, i, 0)),
            pl.BlockSpec((gps, 2, n, LANES), p4),
            pl.BlockSpec((gps, SUBLANES, 2 * n), p3),
            pl.BlockSpec((gps, 2, n, m), p4),
            _resident((m, LANES), lambda i: (0, 0)),
            pl.BlockSpec((gps, m, 2 * n), p3),
            pl.BlockSpec((gps, m, 2 * n), p3),
            pl.BlockSpec((gps, m, 2 * n), p3),
            pl.BlockSpec((gps, m, LANES), p3),
        ],
        out_specs=pl.BlockSpec((chunk, gps * m, r), lambda i: (0, i, 0)),
        out_shape=jax.ShapeDtypeStruct((chunk, g * m, r), BF16),
        scratch_shapes=[pltpu.VMEM((gps, lm, lm + 2 * n), BF16)],
        compiler_params=_cparams(("parallel",)),
        name="s5scan",
    )(ug, lamc, lamr, bb, rep, cc, ca, cbm, dsk)


def _out1_kernel(y_ref, gate_ref, x_ref, wg_ref, wo_ref, g_ref, b_ref, o_ref, v_ref, s_ref, *, alpha):
    w = y_ref.shape[1]
    d = x_ref.shape[1]
    mc = 512
    y = y_ref[0]
    for c0 in range(0, w, mc):
        ga = jnp.dot(wg_ref[c0:c0 + mc, :], y, preferred_element_type=F32)
        gb = jnp.dot(wg_ref[w + c0:w + c0 + mc, :], y, preferred_element_type=F32)
        v_ref[c0:c0 + mc, :] = (ga * _sigmoid(gb) * _silu(gate_ref[0, c0:c0 + mc, :].astype(F32))).astype(BF16)
    v = v_ref[...]
    for c0 in range(0, d, mc):
        s_ref[c0:c0 + mc, :] = alpha * x_ref[0, c0:c0 + mc, :] + jnp.dot(
            wo_ref[c0:c0 + mc, :], v, preferred_element_type=F32)
    s = s_ref[...]
    mu = jnp.mean(s, axis=0, keepdims=True)
    sc = s - mu
    var = jnp.mean(sc * sc, axis=0, keepdims=True)
    o_ref[0] = sc * lax.rsqrt(var + LN_EPS) * g_ref[...] + b_ref[...]


def _out1(y3, ug, xt, wgt, wot, g_col, b_col, *, alpha):
    l, w, r = y3.shape
    d = xt.shape[1]
    gate_blk = 1
    return pl.pallas_call(
        functools.partial(_out1_kernel, alpha=alpha),
        grid=(l,),
        in_specs=[
            pl.BlockSpec((1, w, r), lambda i: (i, 0, 0)),
            pl.BlockSpec((1, w, r), lambda i: (i, gate_blk, 0)),
            pl.BlockSpec((1, d, r), lambda i: (i, 0, 0)),
            _resident((2 * w, w), lambda i: (0, 0)),
            _resident((d, w), lambda i: (0, 0)),
            _resident((d, 1), lambda i: (0, 0)),
            _resident((d, 1), lambda i: (0, 0)),
        ],
        out_specs=pl.BlockSpec((1, d, r), lambda i: (i, 0, 0)),
        out_shape=jax.ShapeDtypeStruct((l, d, r), F32),
        scratch_shapes=[pltpu.VMEM((w, r), BF16), pltpu.VMEM((d, r), F32)],
        compiler_params=_cparams(("parallel",)),
        name="out1",
    )(y3, ug, xt, wgt, wot, g_col, b_col)


def _pad_rows(v, rows=SUBLANES):
    v = jnp.atleast_2d(v.astype(F32))
    return jnp.pad(v, ((0, rows - v.shape[0]), (0, 0)))


def _pad_lanes(v, lanes=LANES):
    return jnp.pad(v, [(0, 0)] * (v.ndim - 1) + [(0, lanes - v.shape[-1])])


def _layer0(x2, batch, seq, in_w, conv_w, conv_b, dt_bias, a_log, d_skip, norm_g, out_w, ln_g, ln_b, alpha):
    d = x2.shape[1]
    ssd_heads = dt_bias.shape[0]
    width = ssd_heads * SSD_HEAD_DIM
    conv_dim = conv_w.shape[1]
    d_state = (conv_dim - width) // (2 * SSD_GROUPS)
    att_width = (in_w.shape[1] - width - conv_dim - ssd_heads) // 4
    att_heads = att_width // ATT_HEAD_DIM

    z_end = width
    xbc_end = z_end + conv_dim
    dt_end = xbc_end + ssd_heads
    w_all = in_w.astype(BF16)
    w_b = w_all[:, dt_end:]
    w_dt = _pad_lanes(w_all[:, xbc_end:dt_end])
    z_col, xbc_col = 0, z_end
    q_col = xbc_end
    k_col, v_col, g_col = q_col + att_width, q_col + 2 * att_width, q_col + 3 * att_width

    proj, dt_raw = _inproj(x2, w_all, xbc_end, w_b, w_dt, tm=1024, tn=1024)

    expand = (jnp.arange(LANES)[:, None] == (jnp.arange(width)[None, :] // SSD_HEAD_DIM)).astype(BF16)
    ya = _ssd(proj, dt_raw, _pad_rows(conv_w), _pad_rows(conv_b), _pad_rows(_pad_lanes(dt_bias.astype(F32))),
              _pad_rows(_pad_lanes(-jnp.exp(a_log.astype(F32)))),
              _pad_rows(jnp.repeat(d_skip.astype(F32), SSD_HEAD_DIM)), _pad_rows(norm_g), expand,
              batch=batch, seq=seq, width=width, n_groups=SSD_GROUPS, d_state=d_state,
              z_col=z_col, xbc_col=xbc_col)

    half = ROPE_DIM // 2
    inv_freq = ROPE_THETA ** (-(jnp.arange(half, dtype=F32) * 2.0 / ROPE_DIM))
    ang = jnp.arange(seq, dtype=F32)[:, None] * inv_freq[None, :]
    ones = jnp.ones((seq, ATT_HEAD_DIM - ROPE_DIM), F32)
    cos_t = jnp.concatenate([jnp.cos(ang), jnp.cos(ang), ones], axis=1)
    sin_t = jnp.concatenate([-jnp.sin(ang), jnp.sin(ang), 0.0 * ones], axis=1)
    yb = _moba(proj, cos_t, sin_t, batch=batch, seq=seq, heads=att_heads,
               q_col=q_col, k_col=k_col, v_col=v_col, g_col=g_col, heads_per_step=4)

    return _out0(ya, yb, x2, out_w.astype(BF16), _pad_rows(ln_g), _pad_rows(ln_b), alpha=alpha, tm=512)


def _layer1(x2, batch, seq, in_w, lam_re, lam_im, log_dt, b_re, b_im, c_re, c_im, d_skip, glu_w, out_w,
            ln_g, ln_b, alpha):
    t, d = x2.shape
    chunk = S5_L
    cps = seq // chunk
    r = batch * cps
    groups, n = lam_re.shape
    m = S5_GROUP
    w = groups * m

    lre, lim = lam_re.astype(F32), lam_im.astype(F32)
    dt = jnp.exp(log_dt.astype(F32))[:, None]
    mag = jnp.exp(lre * dt)
    lbr, lbi = mag * jnp.cos(lim * dt), mag * jnp.sin(lim * dt)
    den = lre * lre + lim * lim
    fr = ((lbr - 1.0) * lre + lbi * lim) / den
    fi = (lbi * lre - (lbr - 1.0) * lim) / den
    bre, bim = b_re.astype(F32), b_im.astype(F32)
    bbr = fr[..., None] * bre - fi[..., None] * bim
    bbi = fr[..., None] * bim + fi[..., None] * bre
    lamc = jnp.broadcast_to(jnp.stack([lbr, lbi], axis=1)[..., None], (groups, 2, n, LANES))
    lamr = jnp.stack([jnp.concatenate([lbr, lbr], -1), jnp.concatenate([lbi, lbi], -1)], axis=1)
    lamr = jnp.pad(lamr, ((0, 0), (0, SUBLANES - 2), (0, 0)))
    bb = jnp.stack([bbr, bbi], axis=1)
    rep = (jnp.arange(m)[:, None] == (jnp.arange(LANES)[None, :] % m)).astype(BF16)
    cr, ci = c_re.astype(F32), c_im.astype(F32)
    cc = jnp.concatenate([cr, -ci], axis=-1)
    cbm = jnp.concatenate([-ci, -cr], axis=-1)
    dsk = jnp.broadcast_to(d_skip.astype(F32).reshape(groups, m, 1), (groups, m, LANES))

    xt = x2.reshape(r, chunk, d).transpose(1, 2, 0)
    ug = _inproj1(xt, in_w.T.astype(BF16))
    y3 = _s5(ug, lamc, lamr, bb, rep, cc, cc, cbm, dsk, chunks_per_seq=cps)
    o3 = _out1(y3, ug, xt, glu_w.T.astype(BF16), out_w.T.astype(BF16),
               ln_g.astype(F32)[:, None], ln_b.astype(F32)[:, None], alpha=alpha)
    return o3.transpose(2, 0, 1).reshape(t, d)


def kernel(x, in0_w, conv_w, conv_b, dt_bias, a_log, ssd_d, ssd_norm_g, out0_w, in1_w, s5_lam_re, s5_lam_im,
           s5_log_dt, s5_b_re, s5_b_im, s5_c_re, s5_c_im, s5_d, glu_w, out1_w, ln_g, ln_b):
    batch, seq, d = x.shape
    depth = ln_g.shape[0]
    alpha = (2 * depth) ** 0.25
    x2 = x.reshape(batch * seq, d)
    for layer in range(depth):
        i = layer // 2
        if layer % 2 == 0:
            x2 = _layer0(x2, batch, seq, in0_w[i], conv_w[i], conv_b[i], dt_bias[i], a_log[i], ssd_d[i],
                         ssd_norm_g[i], out0_w[i], ln_g[layer], ln_b[layer], alpha)
        else:
            x2 = _layer1(x2, batch, seq, in1_w[i], s5_lam_re[i], s5_lam_im[i], s5_log_dt[i], s5_b_re[i],
                         s5_b_im[i], s5_c_re[i], s5_c_im[i], s5_d[i], glu_w[i], out1_w[i],
                         ln_g[layer], ln_b[layer], alpha)
    return x2.reshape(batch, seq, d).astype(x.dtype)
```

```python
import functools
import math

import jax
import jax.numpy as jnp
from jax import lax
from jax.experimental import pallas as pl
from jax.experimental.pallas import tpu as pltpu

F32 = jnp.float32
BF16 = jnp.bfloat16

SSD_HEAD_DIM = 64
SSD_GROUPS = 4
SSD_STATE = 128
SSD_CONV = 4
SSD_CHUNK = 256
ATT_HEAD_DIM = 128
MOBA_BLOCK = 256
MOBA_TOPK = 3
ROPE_THETA = 500000.0
ROPE_DIM = ATT_HEAD_DIM // 4
S5_GROUP = 16
S5_STATE = 64
LN_EPS = 1e-5
RMS_EPS = 1e-5
NEG_INF = -1e30

LANES = 128
SUBLANES = 8
VMEM_LIMIT = 56 * 1024 * 1024

S5_L = 64


def _cparams(sem):
    return pltpu.CompilerParams(dimension_semantics=sem, vmem_limit_bytes=VMEM_LIMIT)


def _resident(shape, index_map):
    return pl.BlockSpec(shape, index_map, pipeline_mode=pl.Buffered(1))


def _sigmoid(v):
    return 0.5 + 0.5 * jnp.tanh(0.5 * v)


def _silu(v):
    h = 0.5 * v
    return h + h * jnp.tanh(h)


def _nt_dot(a, b):
    return lax.dot_general(a, b, (((1,), (1,)), ((), ())), preferred_element_type=F32)


def _tn_dot(a, b):
    return lax.dot_general(a, b, (((0,), (0,)), ((), ())), preferred_element_type=F32)


def _dot_hi(a, b):
    a_hi = a.astype(BF16)
    b_hi = b.astype(BF16)
    a_lo = (a - a_hi.astype(F32)).astype(BF16)
    b_lo = (b - b_hi.astype(F32)).astype(BF16)
    return (jnp.dot(a_hi, b_hi, preferred_element_type=F32) + jnp.dot(a_hi, b_lo, preferred_element_type=F32)
            + jnp.dot(a_lo, b_hi, preferred_element_type=F32))


def _dot_split(a, b, passes, split_lhs=True):
    rem = a if split_lhs else b
    acc = None
    for _ in range(passes):
        piece = rem.astype(BF16)
        part = jnp.dot(piece, b, preferred_element_type=F32) if split_lhs else jnp.dot(
            a, piece, preferred_element_type=F32)
        acc = part if acc is None else acc + part
        rem = rem - piece.astype(F32)
    return acc


def _inproj_kernel(x_ref, wa_ref, wb_ref, wdt_ref, o_ref, dt_ref, xb_ref, *, tiles_a):
    j = pl.program_id(1)

    @pl.when(j == 0)
    def _():
        xb = x_ref[...].astype(BF16)
        xb_ref[...] = xb
        dt_ref[...] = _nt_dot(xb, wdt_ref[...])

    @pl.when(j < tiles_a)
    def _():
        o_ref[...] = _nt_dot(xb_ref[...], wa_ref[...]).astype(o_ref.dtype)

    @pl.when(j >= tiles_a)
    def _():
        o_ref[...] = _nt_dot(xb_ref[...], wb_ref[...]).astype(o_ref.dtype)


def _inproj(x2, w_a, cols_a, w_b, w_dt, tm, tn):
    t, d = x2.shape
    assert cols_a % tn == 0 and w_b.shape[0] % tn == 0
    tiles_a, tiles_b = cols_a // tn, w_b.shape[0] // tn
    n = (tiles_a + tiles_b) * tn
    return pl.pallas_call(
        functools.partial(_inproj_kernel, tiles_a=tiles_a),
        grid=(t // tm, tiles_a + tiles_b),
        in_specs=[
            pl.BlockSpec((tm, d), lambda i, j: (i, 0)),
            pl.BlockSpec((tn, d), lambda i, j: (jnp.minimum(j, tiles_a - 1), 0)),
            pl.BlockSpec((tn, d), lambda i, j: (jnp.maximum(j - tiles_a, 0), 0)),
            _resident((LANES, d), lambda i, j: (0, 0)),
        ],
        out_specs=[
            pl.BlockSpec((tm, tn), lambda i, j: (i, j)),
            pl.BlockSpec((tm, LANES), lambda i, j: (i, 0)),
        ],
        out_shape=[jax.ShapeDtypeStruct((t, n), BF16), jax.ShapeDtypeStruct((t, LANES), F32)],
        scratch_shapes=[pltpu.VMEM((tm, d), BF16)],
        compiler_params=_cparams(("parallel", "arbitrary")),
        name="inproj0",
    )(x2, w_a, w_b, w_dt)


def _ssd_kernel(z_ref, xs_ref, bm_ref, cm_ref, dt_ref, cw_ref, cb_ref, dtb_ref, aneg_ref, dsk_ref, ng_ref,
                exp_ref, o_ref, xpad_ref, state_ref, act_ref, y_ref, *, n_groups, head_dim):
    c = pl.program_id(1)
    chunk, width = z_ref.shape
    d_state = bm_ref.shape[1] // n_groups
    heads_per_group = width // head_dim // n_groups
    gw = heads_per_group * head_dim

    @pl.when(c == 0)
    def _():
        xpad_ref[0:SUBLANES, :] = jnp.zeros((SUBLANES, xpad_ref.shape[1]), F32)
        state_ref[...] = jnp.zeros_like(state_ref)

    strip = 512
    col0 = 0
    for src in (xs_ref, bm_ref, cm_ref):
        for s0 in range(0, src.shape[1], strip):
            cols = slice(col0 + s0, col0 + s0 + strip)
            xs_ = src[:, s0:s0 + strip].astype(F32)
            xpad_ref[SUBLANES:, cols] = xs_
            acc = cb_ref[0:1, cols] + cw_ref[SSD_CONV - 1:SSD_CONV, cols] * xs_
            for k in range(1, SSD_CONV):
                acc = acc + cw_ref[SSD_CONV - 1 - k:SSD_CONV - k, cols] * xpad_ref[SUBLANES - k:SUBLANES - k + chunk, cols]
            act_ref[:, cols] = _silu(acc)
            xpad_ref[0:SUBLANES, cols] = xs_[chunk - SUBLANES:, :]
        col0 += src.shape[1]

    dt_in = dt_ref[...] + dtb_ref[0:1, :]
    dt = jnp.maximum(dt_in, 0.0) + jnp.log(1.0 + jnp.exp(-jnp.abs(dt_in)))
    a = dt * aneg_ref[0:1, :]
    row = lax.broadcasted_iota(jnp.int32, (chunk, chunk), 0)
    col = lax.broadcasted_iota(jnp.int32, (chunk, chunk), 1)
    causal = row >= col
    cs = _dot_split(jnp.where(causal, 1.0, 0.0).astype(BF16), a, 3, split_lhs=False)
    cs2 = cs * math.log2(math.e)
    cs2_t = cs2.T
    cs_last = cs[chunk - 1:chunk, :]
    expand = exp_ref[...]
    dt_x = _dot_split(dt, expand, 2)
    ecs_x = _dot_split(jnp.exp(cs), expand, 2)
    dec_x = _dot_split(jnp.exp(cs_last - cs), expand, 2)
    sdec_x = _dot_split(jnp.broadcast_to(jnp.exp(cs_last), (SUBLANES, LANES)), expand, 2)[0:1, :]

    lane = lax.broadcasted_iota(jnp.int32, (chunk, LANES), 1)
    first_half = lane < head_dim
    for g in range(n_groups):
        gs = slice(g * gw, (g + 1) * gw)
        b_g = act_ref[:, width + g * d_state:width + (g + 1) * d_state].astype(BF16)
        c_g = act_ref[:, width + (n_groups + g) * d_state:width + (n_groups + g + 1) * d_state].astype(BF16)
        xs_g = act_ref[:, gs]
        xdt = xs_g * dt_x[:, gs]
        xdt_b = xdt.astype(BF16)
        cb = jnp.where(causal, _nt_dot(c_g, b_g), 0.0)
        st = state_ref[:, gs]
        y_g = jnp.dot(c_g, st.astype(BF16), preferred_element_type=F32) * ecs_x[:, gs] + dsk_ref[0:1, gs] * xs_g
        state_ref[:, gs] = st * sdec_x[:, gs] + _tn_dot(b_g, (xdt * dec_x[:, gs]).astype(BF16))
        for pr in range(heads_per_group // 2):
            xp = xdt_b[:, pr * LANES:(pr + 1) * LANES]
            halves = []
            for half in range(2):
                h = g * heads_per_group + 2 * pr + half
                seg = cs2[:, h:h + 1] - cs2_t[h:h + 1, :]
                m = (cb * jnp.exp2(jnp.minimum(seg, 0.0))).astype(BF16)
                halves.append(jnp.dot(m, xp, preferred_element_type=F32))
            y_pair = jnp.where(first_half, halves[0], halves[1])
            y_ref[:, g * gw + pr * LANES:g * gw + (pr + 1) * LANES] = y_pair + y_g[:, pr * LANES:(pr + 1) * LANES]

    yz = y_ref[...] * _silu(z_ref[...].astype(F32))
    ms = jnp.mean(yz * yz, axis=1, keepdims=True)
    o_ref[...] = (yz * lax.rsqrt(ms + RMS_EPS) * ng_ref[0:1, :]).astype(o_ref.dtype)


def _ssd(proj, dt_raw, cw, cb, dtb, aneg, dsk, ng, expand, *, batch, seq, width, n_groups, d_state,
         z_col, xbc_col):
    conv_dim = width + 2 * n_groups * d_state
    bc_w = n_groups * d_state
    nc = seq // SSD_CHUNK
    t = batch * seq
    kern = functools.partial(_ssd_kernel, n_groups=n_groups, head_dim=SSD_HEAD_DIM)
    b_col, c_col = xbc_col + width, xbc_col + width + bc_w
    assert z_col % width == 0 and xbc_col % width == 0 and b_col % bc_w == 0 and c_col % bc_w == 0
    const = lambda b, c: (0, 0)
    return pl.pallas_call(
        kern,
        grid=(batch, nc),
        in_specs=[
            pl.BlockSpec((SSD_CHUNK, width), lambda b, c: (b * nc + c, z_col // width)),
            pl.BlockSpec((SSD_CHUNK, width), lambda b, c: (b * nc + c, xbc_col // width)),
            pl.BlockSpec((SSD_CHUNK, bc_w), lambda b, c: (b * nc + c, b_col // bc_w)),
            pl.BlockSpec((SSD_CHUNK, bc_w), lambda b, c: (b * nc + c, c_col // bc_w)),
            pl.BlockSpec((SSD_CHUNK, LANES), lambda b, c: (b * nc + c, 0)),
            _resident((SUBLANES, conv_dim), const),
            _resident((SUBLANES, conv_dim), const),
            _resident((SUBLANES, LANES), const),
            _resident((SUBLANES, LANES), const),
            _resident((SUBLANES, width), const),
            _resident((SUBLANES, width), const),
            _resident((LANES, width), const),
        ],
        out_specs=pl.BlockSpec((SSD_CHUNK, width), lambda b, c: (b * nc + c, 0)),
        out_shape=jax.ShapeDtypeStruct((t, width), BF16),
        scratch_shapes=[
            pltpu.VMEM((SUBLANES + SSD_CHUNK, conv_dim), F32),
            pltpu.VMEM((d_state, width), F32),
            pltpu.VMEM((SSD_CHUNK, conv_dim), F32),
            pltpu.VMEM((SSD_CHUNK, width), F32),
        ],
        compiler_params=_cparams(("parallel", "arbitrary")),
        name="ssd0",
    )(proj, proj, proj, proj, dt_raw, cw, cb, dtb, aneg, dsk, ng, expand)


def _rope(v, cos, sin):
    half = ROPE_DIM // 2
    lane = lax.broadcasted_iota(jnp.int32, v.shape, 1)
    partner = jnp.where(lane < half, pltpu.roll(v, LANES - half, 1), pltpu.roll(v, half, 1))
    return v * cos + partner * sin


def _moba_kernel(q_ref, k_ref, v_ref, g_ref, cos_ref, sin_ref, o_ref, kr_ref, vt_ref, km_ref, ch_ref,
                 sa_ref, sb_ref, *, n_blocks, heads_per_step):
    qb = pl.program_id(2)
    blk = MOBA_BLOCK
    hd = ATT_HEAD_DIM
    scale = hd ** -0.5 * math.log2(math.e)
    heads = range(heads_per_step)

    n_rows = km_ref.shape[1]

    @pl.when(qb == 0)
    def _():
        km_ref[...] = jnp.zeros_like(km_ref)
        for n in range(n_blocks):
            rows = slice(n * blk, (n + 1) * blk)
            for j in heads:
                cols = slice(j * hd, (j + 1) * hd)
                kr = _rope(k_ref[rows, cols].astype(F32), cos_ref[rows, :], sin_ref[rows, :])
                kr_ref[j, rows, :] = kr.astype(BF16)
                if n + 1 < n_rows:
                    km_ref[j, n + 1:n + 2, :] = jnp.mean(kr, axis=0, keepdims=True)
                vt_ref[j, 0:hd, rows] = v_ref[rows, cols].astype(F32).T.astype(BF16)
                vt_ref[j, hd:, rows] = jnp.ones((vt_ref.shape[1] - hd, blk), BF16)

    q0 = pl.multiple_of(qb * blk, blk)
    cos_q = cos_ref[pl.ds(q0, blk), :]
    sin_q = sin_ref[pl.ds(q0, blk), :]
    row_id = lax.broadcasted_iota(jnp.int32, (n_rows, blk), 0)
    past = (row_id >= 1) & (row_id <= qb)
    key_i = lax.broadcasted_iota(jnp.int32, (blk, blk), 0)
    qry_i = lax.broadcasted_iota(jnp.int32, (blk, blk), 1)

    qs_t = []
    for j in heads:
        qr_t = _rope(q_ref[:, j * hd:(j + 1) * hd].astype(F32), cos_q, sin_q).T
        qs_t.append((qr_t * scale).astype(BF16))

        s_own = jnp.dot(kr_ref[j, pl.ds(q0, blk), :], qs_t[j], preferred_element_type=F32)
        sa_ref[j] = jnp.where(key_i <= qry_i, s_own, NEG_INF)

        gate = jnp.where(past, _dot_hi(km_ref[j], qr_t), NEG_INF)
        rank = jnp.zeros((n_rows, blk), jnp.int32)
        for n2 in range(1, n_blocks):
            other = gate[n2:n2 + 1, :]
            rank = rank + jnp.where(other > gate, 1,
                                    jnp.where(other == gate, jnp.where(row_id > n2, 1, 0), 0))
        ch_ref[j] = jnp.where(row_id == 0, 1.0, jnp.where(past, jnp.where(rank < MOBA_TOPK, 1.0, 0.0), 0.0))

    def stage(k, ref):
        n0 = pl.multiple_of(jnp.clip(k - 1, 0, n_blocks - 1) * blk, blk)
        for j in heads:
            ref[j] = jnp.dot(kr_ref[j, pl.ds(n0, blk), :], qs_t[j], preferred_element_type=F32)

    def consume(k, ref, carry):
        n0 = pl.multiple_of(jnp.where(k == 0, qb, k - 1) * blk, blk)
        out = []
        for j in heads:
            m, acc = carry[j]
            taken = ch_ref[j, pl.ds(k, 1), :] > 0.5
            m_new = jnp.maximum(m, jnp.where(taken, jnp.max(ref[j], axis=0, keepdims=True), NEG_INF))
            alpha = jnp.exp2(m - m_new)
            p = jnp.exp2(ref[j] - jnp.where(taken, m_new, -NEG_INF)).astype(BF16)
            out.append((m_new, alpha * acc + jnp.dot(vt_ref[j, :, pl.ds(n0, blk)], p,
                                                     preferred_element_type=F32)))
        return tuple(out)

    def two_blocks(i, carry):
        stage(2 * i + 1, sb_ref)
        carry = consume(2 * i, sa_ref, carry)
        stage(2 * i + 2, sa_ref)
        return consume(2 * i + 1, sb_ref, carry)

    init = tuple((jnp.full((1, blk), NEG_INF, F32), jnp.zeros((vt_ref.shape[1], blk), F32)) for _ in heads)
    final = lax.fori_loop(0, qb // 2 + 1, two_blocks, init)
    for j in heads:
        _, acc = final[j]
        cols = slice(j * hd, (j + 1) * hd)
        out_t = acc[0:hd, :] / acc[hd:hd + 1, :]
        o_ref[:, cols] = (out_t.T * _silu(g_ref[:, cols].astype(F32))).astype(o_ref.dtype)


def _moba(proj, cos_t, sin_t, *, batch, seq, heads, q_col, k_col, v_col, g_col, heads_per_step):
    nb = seq // MOBA_BLOCK
    hd = ATT_HEAD_DIM
    hw = hd * heads_per_step
    t = batch * seq
    assert heads % heads_per_step == 0 and all(c % hw == 0 for c in (q_col, k_col, v_col, g_col))
    qc, kc, vc, gc = (c // hw for c in (q_col, k_col, v_col, g_col))
    kern = functools.partial(_moba_kernel, n_blocks=nb, heads_per_step=heads_per_step)
    return pl.pallas_call(
        kern,
        grid=(batch, heads // heads_per_step, nb),
        in_specs=[
            pl.BlockSpec((MOBA_BLOCK, hw), lambda b, h, i: (b * nb + i, qc + h)),
            pl.BlockSpec((seq, hw), lambda b, h, i: (b, kc + h)),
            pl.BlockSpec((seq, hw), lambda b, h, i: (b, vc + h)),
            pl.BlockSpec((MOBA_BLOCK, hw), lambda b, h, i: (b * nb + i, gc + h)),
            _resident((seq, hd), lambda b, h, i: (0, 0)),
            _resident((seq, hd), lambda b, h, i: (0, 0)),
        ],
        out_specs=pl.BlockSpec((MOBA_BLOCK, hw), lambda b, h, i: (b * nb + i, h)),
        out_shape=jax.ShapeDtypeStruct((t, heads * hd), BF16),
        scratch_shapes=[
            pltpu.VMEM((heads_per_step, seq, hd), BF16),
            pltpu.VMEM((heads_per_step, hd + 16, seq), BF16),
            pltpu.VMEM((heads_per_step, nb, hd), F32),
            pltpu.VMEM((heads_per_step, nb, MOBA_BLOCK), F32),
            pltpu.VMEM((heads_per_step, MOBA_BLOCK, MOBA_BLOCK), F32),
            pltpu.VMEM((heads_per_step, MOBA_BLOCK, MOBA_BLOCK), F32),
        ],
        compiler_params=_cparams(("parallel", "parallel", "arbitrary")),
        name="moba0",
    )(proj, proj, proj, proj, cos_t, sin_t)


def _out0_kernel(ya_ref, yb_ref, x_ref, w_ref, g_ref, b_ref, o_ref, *, alpha):
    ka = ya_ref.shape[1]
    h = jnp.dot(ya_ref[...], w_ref[0:ka, :], preferred_element_type=F32)
    h = h + jnp.dot(yb_ref[...], w_ref[ka:, :], preferred_element_type=F32)
    v = alpha * x_ref[...] + h
    mu = jnp.mean(v, axis=1, keepdims=True)
    vc = v - mu
    var = jnp.mean(vc * vc, axis=1, keepdims=True)
    o_ref[...] = vc * lax.rsqrt(var + LN_EPS) * g_ref[0:1, :] + b_ref[0:1, :]


def _out0(ya, yb, x2, w, g, b, *, alpha, tm):
    t, d = x2.shape
    ka, kb = ya.shape[1], yb.shape[1]
    return pl.pallas_call(
        functools.partial(_out0_kernel, alpha=alpha),
        grid=(t // tm,),
        in_specs=[
            pl.BlockSpec((tm, ka), lambda i: (i, 0)),
            pl.BlockSpec((tm, kb), lambda i: (i, 0)),
            pl.BlockSpec((tm, d), lambda i: (i, 0)),
            _resident((ka + kb, d), lambda i: (0, 0)),
            _resident((SUBLANES, d), lambda i: (0, 0)),
            _resident((SUBLANES, d), lambda i: (0, 0)),
        ],
        out_specs=pl.BlockSpec((tm, d), lambda i: (i, 0)),
        out_shape=jax.ShapeDtypeStruct((t, d), F32),
        compiler_params=_cparams(("parallel",)),
        name="out0",
    )(ya, yb, x2, w, g, b)


def _inproj1_kernel(x_ref, w_ref, o_ref):
    o_ref[0] = jnp.dot(w_ref[...], x_ref[0].astype(BF16), preferred_element_type=F32).astype(o_ref.dtype)


def _inproj1(xt, wt):
    l, d, r = xt.shape
    n = wt.shape[0]
    return pl.pallas_call(
        _inproj1_kernel,
        grid=(l,),
        in_specs=[pl.BlockSpec((1, d, r), lambda i: (i, 0, 0)), _resident((n, d), lambda i: (0, 0))],
        out_specs=pl.BlockSpec((1, n, r), lambda i: (i, 0, 0)),
        out_shape=jax.ShapeDtypeStruct((l, n, r), BF16),
        compiler_params=_cparams(("parallel",)),
        name="inproj1",
    )(xt, wt)


def _cpow(lr, li, d, nbits):
    pr = jnp.ones(d.shape, F32)
    pi = jnp.zeros(d.shape, F32)
    br, bi = lr, li
    for bit in range(nbits):
        on = ((d >> bit) & 1) == 1
        nr = pr * br - pi * bi
        ni = pr * bi + pi * br
        pr = jnp.where(on, nr, pr)
        pi = jnp.where(on, ni, pi)
        if bit + 1 < nbits:
            br, bi = br * br - bi * bi, 2.0 * br * bi
    return pr, pi


def _gelu_tanh(v):
    c = math.sqrt(2.0 / math.pi)
    h = 0.5 * v
    return h + h * jnp.tanh(v * (c + (c * 0.044715) * (v * v)))


def _s5_group(j, u_ref, lamc_ref, lamr_ref, bb_ref, rep_ref, cc_ref, ca_ref, cbm_ref, dsk_ref, o_ref, tz_ref,
              *, chunk, chunks_per_seq):
    m = S5_GROUP
    n = S5_STATE
    lm = chunk * m
    r = u_ref.shape[2]
    u = u_ref[:, j * m:(j + 1) * m, :].reshape(lm, r)

    pos_per_blk = LANES // m
    n_lane_blk = lm // LANES
    lam_r, lam_i = lamc_ref[j, 0], lamc_ref[j, 1]
    lane = lax.broadcasted_iota(jnp.int32, (n, LANES), 1)
    pr, pi = _cpow(lam_r, lam_i, (pos_per_blk - 1) - lane // m, (pos_per_blk - 1).bit_length())
    lbr, lbi = lam_r, lam_i
    for _ in range(pos_per_blk.bit_length() - 1):
        lbr, lbi = lbr * lbr - lbi * lbi, 2.0 * lbr * lbi
    bbr = _dot_split(bb_ref[j, 0], rep_ref[...], 3)
    bbi = _dot_split(bb_ref[j, 1], rep_ref[...], 3)
    bl_r, bl_i = [None] * n_lane_blk, [None] * n_lane_blk
    for c in range(n_lane_blk - 1, -1, -1):
        bl_r[c] = pr * bbr - pi * bbi
        bl_i[c] = pr * bbi + pi * bbr
        pr, pi = pr * lbr - pi * lbi, pr * lbi + pi * lbr
    bl = jnp.concatenate([jnp.concatenate(bl_r, axis=1), jnp.concatenate(bl_i, axis=1)], axis=0)

    krev = _dot_hi(cc_ref[j], bl)
    col = lax.broadcasted_iota(jnp.int32, (m, lm), 1)
    row = lax.broadcasted_iota(jnp.int32, (m, lm), 0)
    d_lanes = jnp.concatenate([dsk_ref[j]] * n_lane_blk, axis=1)
    krev = krev + jnp.where(col == (chunk - 1) * m + row, d_lanes, 0.0)

    col_s = col // m
    for t in range(chunk):
        rolled = pltpu.roll(krev, ((t + 1) * m) % lm, 1)
        tz_ref[j, t * m:(t + 1) * m, 0:lm] = jnp.where(col_s <= t, rolled, 0.0).astype(BF16)

    e = jnp.dot(bl.astype(BF16), u, preferred_element_type=F32)
    lane_r = lax.broadcasted_iota(jnp.int32, (n, r), 1) % chunks_per_seq
    hr = jnp.where(lane_r >= 1, pltpu.roll(e[:n], 1, 1), 0.0)
    hi = jnp.where(lane_r >= 1, pltpu.roll(e[n:], 1, 1), 0.0)
    mr, mi = lbr, lbi
    for _ in range((chunk // pos_per_blk).bit_length() - 1):
        mr, mi = mr * mr - mi * mi, 2.0 * mr * mi
    mr = jnp.concatenate([mr] * (r // LANES), axis=1) if r > LANES else mr[:, :r]
    mi = jnp.concatenate([mi] * (r // LANES), axis=1) if r > LANES else mi[:, :r]
    step = 1
    while step < chunks_per_seq:
        sr = jnp.where(lane_r >= step, pltpu.roll(hr, step, 1), 0.0)
        si = jnp.where(lane_r >= step, pltpu.roll(hi, step, 1), 0.0)
        hr, hi = hr + mr * sr - mi * si, hi + mr * si + mi * sr
        mr, mi = mr * mr - mi * mi, 2.0 * mr * mi
        step *= 2
    h_in = jnp.concatenate([hr, hi], axis=0).astype(BF16)

    lam_rr = jnp.broadcast_to(lamr_ref[j, 0:1, :], (m, 2 * n))
    lam_ri = jnp.broadcast_to(lamr_ref[j, 1:2, :], (m, 2 * n))
    ca, cbm = ca_ref[j], cbm_ref[j]
    qr, qi = lam_rr, lam_ri
    for t in range(chunk):
        tz_ref[j, t * m:(t + 1) * m, lm:] = (ca * qr + cbm * qi).astype(BF16)
        if t + 1 < chunk:
            qr, qi = qr * lam_rr - qi * lam_ri, qr * lam_ri + qi * lam_rr

    y = jnp.dot(tz_ref[j], jnp.concatenate([u, h_in], axis=0), preferred_element_type=F32)
    o_ref[:, j * m:(j + 1) * m, :] = _gelu_tanh(y).astype(o_ref.dtype).reshape(chunk, m, r)


def _s5_kernel(*refs, chunk, chunks_per_seq, groups_per_step):
    for j in range(groups_per_step):
        _s5_group(j, *refs, chunk=chunk, chunks_per_seq=chunks_per_seq)


def _s5(ug, lamc, lamr, bb, rep, cc, ca, cbm, dsk, *, chunks_per_seq):
    chunk, _, r = ug.shape
    g = lamc.shape[0]
    m, n = S5_GROUP, S5_STATE
    lm = chunk * m
    gps = 2
    kern = functools.partial(_s5_kernel, chunk=chunk, chunks_per_seq=chunks_per_seq, groups_per_step=gps)
    p4 = lambda i: (i, 0, 0, 0)
    p3 = lambda i: (i, 0, 0)
    return pl.pallas_call(
        kern,
        grid=(g // gps,),
        in_specs=[
            pl.BlockSpec((chunk, gps * m, r), lambda i: (0, i, 0)),
            pl.BlockSpec((gps, 2, n, LANES), p4),
            pl.BlockSpec((gps, SUBLANES, 2 * n), p3),
            pl.BlockSpec((gps, 2, n, m), p4),
            _resident((m, LANES), lambda i: (0, 0)),
            pl.BlockSpec((gps, m, 2 * n), p3),
            pl.BlockSpec((gps, m, 2 * n), p3),
            pl.BlockSpec((gps, m, 2 * n), p3),
            pl.BlockSpec((gps, m, LANES), p3),
        ],
        out_specs=pl.BlockSpec((chunk, gps * m, r), lambda i: (0, i, 0)),
        out_shape=jax.ShapeDtypeStruct((chunk, g * m, r), BF16),
        scratch_shapes=[pltpu.VMEM((gps, lm, lm + 2 * n), BF16)],
        compiler_params=_cparams(("parallel",)),
        name="s5scan",
    )(ug, lamc, lamr, bb, rep, cc, ca, cbm, dsk)


def _out1_kernel(y_ref, gate_ref, x_ref, wg_ref, wo_ref, g_ref, b_ref, o_ref, v_ref, s_ref, *, alpha):
    w = y_ref.shape[1]
    d = x_ref.shape[1]
    mc = 512
    y = y_ref[0]
    for c0 in range(0, w, mc):
        ga = jnp.dot(wg_ref[c0:c0 + mc, :], y, preferred_element_type=F32)
        gb = jnp.dot(wg_ref[w + c0:w + c0 + mc, :], y, preferred_element_type=F32)
        v_ref[c0:c0 + mc, :] = (ga * _sigmoid(gb) * _silu(gate_ref[0, c0:c0 + mc, :].astype(F32))).astype(BF16)
    v = v_ref[...]
    for c0 in range(0, d, mc):
        s_ref[c0:c0 + mc, :] = alpha * x_ref[0, c0:c0 + mc, :] + jnp.dot(
            wo_ref[c0:c0 + mc, :], v, preferred_element_type=F32)
    s = s_ref[...]
    mu = jnp.mean(s, axis=0, keepdims=True)
    sc = s - mu
    var = jnp.mean(sc * sc, axis=0, keepdims=True)
    o_ref[0] = sc * lax.rsqrt(var + LN_EPS) * g_ref[...] + b_ref[...]


def _out1(y3, ug, xt, wgt, wot, g_col, b_col, *, alpha):
    l, w, r = y3.shape
    d = xt.shape[1]
    gate_blk = 1
    return pl.pallas_call(
        functools.partial(_out1_kernel, alpha=alpha),
        grid=(l,),
        in_specs=[
            pl.BlockSpec((1, w, r), lambda i: (i, 0, 0)),
            pl.BlockSpec((1, w, r), lambda i: (i, gate_blk, 0)),
            pl.BlockSpec((1, d, r), lambda i: (i, 0, 0)),
            _resident((2 * w, w), lambda i: (0, 0)),
            _resident((d, w), lambda i: (0, 0)),
            _resident((d, 1), lambda i: (0, 0)),
            _resident((d, 1), lambda i: (0, 0)),
        ],
        out_specs=pl.BlockSpec((1, d, r), lambda i: (i, 0, 0)),
        out_shape=jax.ShapeDtypeStruct((l, d, r), F32),
        scratch_shapes=[pltpu.VMEM((w, r), BF16), pltpu.VMEM((d, r), F32)],
        compiler_params=_cparams(("parallel",)),
        name="out1",
    )(y3, ug, xt, wgt, wot, g_col, b_col)


def _pad_rows(v, rows=SUBLANES):
    v = jnp.atleast_2d(v.astype(F32))
    return jnp.pad(v, ((0, rows - v.shape[0]), (0, 0)))


def _pad_lanes(v, lanes=LANES):
    return jnp.pad(v, [(0, 0)] * (v.ndim - 1) + [(0, lanes - v.shape[-1])])


def _layer0(x2, batch, seq, in_w, conv_w, conv_b, dt_bias, a_log, d_skip, norm_g, out_w, ln_g, ln_b, alpha):
    d = x2.shape[1]
    ssd_heads = dt_bias.shape[0]
    width = ssd_heads * SSD_HEAD_DIM
    conv_dim = conv_w.shape[1]
    d_state = (conv_dim - width) // (2 * SSD_GROUPS)
    att_width = (in_w.shape[1] - width - conv_dim - ssd_heads) // 4
    att_heads = att_width // ATT_HEAD_DIM

    z_end = width
    xbc_end = z_end + conv_dim
    dt_end = xbc_end + ssd_heads
    w_all = in_w.T.astype(BF16)
    w_b = w_all[dt_end:]
    w_dt = jnp.pad(w_all[xbc_end:dt_end], ((0, LANES - ssd_heads), (0, 0)))
    z_col, xbc_col = 0, z_end
    q_col = xbc_end
    k_col, v_col, g_col = q_col + att_width, q_col + 2 * att_width, q_col + 3 * att_width

    proj, dt_raw = _inproj(x2, w_all, xbc_end, w_b, w_dt, tm=1024, tn=1024)

    expand = (jnp.arange(LANES)[:, None] == (jnp.arange(width)[None, :] // SSD_HEAD_DIM)).astype(BF16)
    ya = _ssd(proj, dt_raw, _pad_rows(conv_w), _pad_rows(conv_b), _pad_rows(_pad_lanes(dt_bias.astype(F32))),
              _pad_rows(_pad_lanes(-jnp.exp(a_log.astype(F32)))),
              _pad_rows(jnp.repeat(d_skip.astype(F32), SSD_HEAD_DIM)), _pad_rows(norm_g), expand,
              batch=batch, seq=seq, width=width, n_groups=SSD_GROUPS, d_state=d_state,
              z_col=z_col, xbc_col=xbc_col)

    half = ROPE_DIM // 2
    inv_freq = ROPE_THETA ** (-(jnp.arange(half, dtype=F32) * 2.0 / ROPE_DIM))
    ang = jnp.arange(seq, dtype=F32)[:, None] * inv_freq[None, :]
    ones = jnp.ones((seq, ATT_HEAD_DIM - ROPE_DIM), F32)
    cos_t = jnp.concatenate([jnp.cos(ang), jnp.cos(ang), ones], axis=1)
    sin_t = jnp.concatenate([-jnp.sin(ang), jnp.sin(ang), 0.0 * ones], axis=1)
    yb = _moba(proj, cos_t, sin_t, batch=batch, seq=seq, heads=att_heads,
               q_col=q_col, k_col=k_col, v_col=v_col, g_col=g_col, heads_per_step=4)

    return _out0(ya, yb, x2, out_w.astype(BF16), _pad_rows(ln_g), _pad_rows(ln_b), alpha=alpha, tm=512)


def _layer1(x2, batch, seq, in_w, lam_re, lam_im, log_dt, b_re, b_im, c_re, c_im, d_skip, glu_w, out_w,
            ln_g, ln_b, alpha):
    t, d = x2.shape
    chunk = S5_L
    cps = seq // chunk
    r = batch * cps
    groups, n = lam_re.shape
    m = S5_GROUP
    w = groups * m

    lre, lim = lam_re.astype(F32), lam_im.astype(F32)
    dt = jnp.exp(log_dt.astype(F32))[:, None]
    mag = jnp.exp(lre * dt)
    lbr, lbi = mag * jnp.cos(lim * dt), mag * jnp.sin(lim * dt)
    den = lre * lre + lim * lim
    fr = ((lbr - 1.0) * lre + lbi * lim) / den
    fi = (lbi * lre - (lbr - 1.0) * lim) / den
    bre, bim = b_re.astype(F32), b_im.astype(F32)
    bbr = fr[..., None] * bre - fi[..., None] * bim
    bbi = fr[..., None] * bim + fi[..., None] * bre
    lamc = jnp.broadcast_to(jnp.stack([lbr, lbi], axis=1)[..., None], (groups, 2, n, LANES))
    lamr = jnp.stack([jnp.concatenate([lbr, lbr], -1), jnp.concatenate([lbi, lbi], -1)], axis=1)
    lamr = jnp.pad(lamr, ((0, 0), (0, SUBLANES - 2), (0, 0)))
    bb = jnp.stack([bbr, bbi], axis=1)
    rep = (jnp.arange(m)[:, None] == (jnp.arange(LANES)[None, :] % m)).astype(BF16)
    cr, ci = c_re.astype(F32), c_im.astype(F32)
    cc = jnp.concatenate([cr, -ci], axis=-1)
    cbm = jnp.concatenate([-ci, -cr], axis=-1)
    dsk = jnp.broadcast_to(d_skip.astype(F32).reshape(groups, m, 1), (groups, m, LANES))

    xt = x2.reshape(r, chunk, d).transpose(1, 2, 0)
    ug = _inproj1(xt, in_w.T.astype(BF16))
    y3 = _s5(ug, lamc, lamr, bb, rep, cc, cc, cbm, dsk, chunks_per_seq=cps)
    o3 = _out1(y3, ug, xt, glu_w.T.astype(BF16), out_w.T.astype(BF16),
               ln_g.astype(F32)[:, None], ln_b.astype(F32)[:, None], alpha=alpha)
    return o3.transpose(2, 0, 1).reshape(t, d)


def kernel(x, in0_w, conv_w, conv_b, dt_bias, a_log, ssd_d, ssd_norm_g, out0_w, in1_w, s5_lam_re, s5_lam_im,
           s5_log_dt, s5_b_re, s5_b_im, s5_c_re, s5_c_im, s5_d, glu_w, out1_w, ln_g, ln_b):
    batch, seq, d = x.shape
    depth = ln_g.shape[0]
    alpha = (2 * depth) ** 0.25
    x2 = x.reshape(batch * seq, d)
    for layer in range(depth):
        i = layer // 2
        if layer % 2 == 0:
            x2 = _layer0(x2, batch, seq, in0_w[i], conv_w[i], conv_b[i], dt_bias[i], a_log[i], ssd_d[i],
                         ssd_norm_g[i], out0_w[i], ln_g[layer], ln_b[layer], alpha)
        else:
            x2 = _layer1(x2, batch, seq, in1_w[i], s5_lam_re[i], s5_lam_im[i], s5_log_dt[i], s5_b_re[i],
                         s5_b_im[i], s5_c_re[i], s5_c_im[i], s5_d[i], glu_w[i], out1_w[i],
                         ln_g[layer], ln_b[layer], alpha)
    return x2.reshape(batch, seq, d).astype(x.dtype)
```

```python
import functools
import math

import jax
import jax.numpy as jnp
from jax import lax
from jax.experimental import pallas as pl
from jax.experimental.pallas import tpu as pltpu

F32 = jnp.float32
BF16 = jnp.bfloat16

SSD_HEAD_DIM = 64
SSD_GROUPS = 4
SSD_STATE = 128
SSD_CONV = 4
SSD_CHUNK = 256
ATT_HEAD_DIM = 128
MOBA_BLOCK = 256
MOBA_TOPK = 3
ROPE_THETA = 500000.0
ROPE_DIM = ATT_HEAD_DIM // 4
S5_GROUP = 16
S5_STATE = 64
LN_EPS = 1e-5
RMS_EPS = 1e-5
NEG_INF = -1e30

LANES = 128
SUBLANES = 8
VMEM_LIMIT = 56 * 1024 * 1024

S5_L = 64


def _cparams(sem):
    return pltpu.CompilerParams(dimension_semantics=sem, vmem_limit_bytes=VMEM_LIMIT)


def _resident(shape, index_map):
    return pl.BlockSpec(shape, index_map, pipeline_mode=pl.Buffered(1))


def _sigmoid(v):
    return 0.5 + 0.5 * jnp.tanh(0.5 * v)


def _silu(v):
    h = 0.5 * v
    return h + h * jnp.tanh(h)


def _nt_dot(a, b):
    return lax.dot_general(a, b, (((1,), (1,)), ((), ())), preferred_element_type=F32)


def _tn_dot(a, b):
    return lax.dot_general(a, b, (((0,), (0,)), ((), ())), preferred_element_type=F32)


def _dot_hi(a, b):
    a_hi = a.astype(BF16)
    b_hi = b.astype(BF16)
    a_lo = (a - a_hi.astype(F32)).astype(BF16)
    b_lo = (b - b_hi.astype(F32)).astype(BF16)
    return (jnp.dot(a_hi, b_hi, preferred_element_type=F32) + jnp.dot(a_hi, b_lo, preferred_element_type=F32)
            + jnp.dot(a_lo, b_hi, preferred_element_type=F32))


def _dot_split(a, b, passes, split_lhs=True):
    rem = a if split_lhs else b
    acc = None
    for _ in range(passes):
        piece = rem.astype(BF16)
        part = jnp.dot(piece, b, preferred_element_type=F32) if split_lhs else jnp.dot(
            a, piece, preferred_element_type=F32)
        acc = part if acc is None else acc + part
        rem = rem - piece.astype(F32)
    return acc


def _inproj_kernel(x_ref, w_ref, wdt_ref, o_ref, dt_ref, xb_ref):
    @pl.when(pl.program_id(1) == 0)
    def _():
        xb = x_ref[...].astype(BF16)
        xb_ref[...] = xb
        dt_ref[...] = _nt_dot(xb, wdt_ref[...])

    o_ref[...] = _nt_dot(xb_ref[...], w_ref[...]).astype(o_ref.dtype)


def _inproj(x2, w_t, w_dt_t, tm, tn):
    t, d = x2.shape
    n = w_t.shape[0]
    return pl.pallas_call(
        _inproj_kernel,
        grid=(t // tm, n // tn),
        in_specs=[
            pl.BlockSpec((tm, d), lambda i, j: (i, 0)),
            pl.BlockSpec((tn, d), lambda i, j: (j, 0)),
            _resident((LANES, d), lambda i, j: (0, 0)),
        ],
        out_specs=[
            pl.BlockSpec((tm, tn), lambda i, j: (i, j)),
            pl.BlockSpec((tm, LANES), lambda i, j: (i, 0)),
        ],
        out_shape=[jax.ShapeDtypeStruct((t, n), BF16), jax.ShapeDtypeStruct((t, LANES), F32)],
        scratch_shapes=[pltpu.VMEM((tm, d), BF16)],
        compiler_params=_cparams(("parallel", "arbitrary")),
        name="inproj0",
    )(x2, w_t, w_dt_t)


def _ssd_kernel(z_ref, xs_ref, bm_ref, cm_ref, dt_ref, cw_ref, cb_ref, dtb_ref, aneg_ref, dsk_ref, ng_ref,
                exp_ref, o_ref, xpad_ref, state_ref, act_ref, y_ref, *, n_groups, head_dim):
    c = pl.program_id(1)
    chunk, width = z_ref.shape
    d_state = bm_ref.shape[1] // n_groups
    heads_per_group = width // head_dim // n_groups
    gw = heads_per_group * head_dim

    @pl.when(c == 0)
    def _():
        xpad_ref[0:SUBLANES, :] = jnp.zeros((SUBLANES, xpad_ref.shape[1]), F32)
        state_ref[...] = jnp.zeros_like(state_ref)

    strip = 512
    col0 = 0
    for src in (xs_ref, bm_ref, cm_ref):
        for s0 in range(0, src.shape[1], strip):
            cols = slice(col0 + s0, col0 + s0 + strip)
            xs_ = src[:, s0:s0 + strip].astype(F32)
            xpad_ref[SUBLANES:, cols] = xs_
            acc = cb_ref[0:1, cols] + cw_ref[SSD_CONV - 1:SSD_CONV, cols] * xs_
            for k in range(1, SSD_CONV):
                acc = acc + cw_ref[SSD_CONV - 1 - k:SSD_CONV - k, cols] * xpad_ref[SUBLANES - k:SUBLANES - k + chunk, cols]
            act_ref[:, cols] = _silu(acc)
            xpad_ref[0:SUBLANES, cols] = xs_[chunk - SUBLANES:, :]
        col0 += src.shape[1]

    dt_in = dt_ref[...] + dtb_ref[0:1, :]
    dt = jnp.maximum(dt_in, 0.0) + jnp.log(1.0 + jnp.exp(-jnp.abs(dt_in)))
    a = dt * aneg_ref[0:1, :]
    row = lax.broadcasted_iota(jnp.int32, (chunk, chunk), 0)
    col = lax.broadcasted_iota(jnp.int32, (chunk, chunk), 1)
    causal = row >= col
    cs = _dot_split(jnp.where(causal, 1.0, 0.0).astype(BF16), a, 3, split_lhs=False)
    cs2 = cs * math.log2(math.e)
    cs2_t = cs2.T
    cs_last = cs[chunk - 1:chunk, :]
    expand = exp_ref[...]
    dt_x = _dot_split(dt, expand, 2)
    ecs_x = _dot_split(jnp.exp(cs), expand, 2)
    dec_x = _dot_split(jnp.exp(cs_last - cs), expand, 2)
    sdec_x = _dot_split(jnp.broadcast_to(jnp.exp(cs_last), (SUBLANES, LANES)), expand, 2)[0:1, :]

    lane = lax.broadcasted_iota(jnp.int32, (chunk, LANES), 1)
    first_half = lane < head_dim
    for g in range(n_groups):
        gs = slice(g * gw, (g + 1) * gw)
        b_g = act_ref[:, width + g * d_state:width + (g + 1) * d_state].astype(BF16)
        c_g = act_ref[:, width + (n_groups + g) * d_state:width + (n_groups + g + 1) * d_state].astype(BF16)
        xs_g = act_ref[:, gs]
        xdt = xs_g * dt_x[:, gs]
        xdt_b = xdt.astype(BF16)
        cb = jnp.where(causal, _nt_dot(c_g, b_g), 0.0)
        st = state_ref[:, gs]
        y_g = jnp.dot(c_g, st.astype(BF16), preferred_element_type=F32) * ecs_x[:, gs] + dsk_ref[0:1, gs] * xs_g
        state_ref[:, gs] = st * sdec_x[:, gs] + _tn_dot(b_g, (xdt * dec_x[:, gs]).astype(BF16))
        for pr in range(heads_per_group // 2):
            xp = xdt_b[:, pr * LANES:(pr + 1) * LANES]
            halves = []
            for half in range(2):
                h = g * heads_per_group + 2 * pr + half
                seg = cs2[:, h:h + 1] - cs2_t[h:h + 1, :]
                m = (cb * jnp.exp2(jnp.minimum(seg, 0.0))).astype(BF16)
                halves.append(jnp.dot(m, xp, preferred_element_type=F32))
            y_pair = jnp.where(first_half, halves[0], halves[1])
            y_ref[:, g * gw + pr * LANES:g * gw + (pr + 1) * LANES] = y_pair + y_g[:, pr * LANES:(pr + 1) * LANES]

    yz = y_ref[...] * _silu(z_ref[...].astype(F32))
    ms = jnp.mean(yz * yz, axis=1, keepdims=True)
    o_ref[...] = (yz * lax.rsqrt(ms + RMS_EPS) * ng_ref[0:1, :]).astype(o_ref.dtype)


def _ssd(proj, dt_raw, cw, cb, dtb, aneg, dsk, ng, expand, *, batch, seq, width, n_groups, d_state,
         z_col, xbc_col):
    conv_dim = width + 2 * n_groups * d_state
    bc_w = n_groups * d_state
    nc = seq // SSD_CHUNK
    t = batch * seq
    kern = functools.partial(_ssd_kernel, n_groups=n_groups, head_dim=SSD_HEAD_DIM)
    b_col, c_col = xbc_col + width, xbc_col + width + bc_w
    assert z_col % width == 0 and xbc_col % width == 0 and b_col % bc_w == 0 and c_col % bc_w == 0
    const = lambda b, c: (0, 0)
    return pl.pallas_call(
        kern,
        grid=(batch, nc),
        in_specs=[
            pl.BlockSpec((SSD_CHUNK, width), lambda b, c: (b * nc + c, z_col // width)),
            pl.BlockSpec((SSD_CHUNK, width), lambda b, c: (b * nc + c, xbc_col // width)),
            pl.BlockSpec((SSD_CHUNK, bc_w), lambda b, c: (b * nc + c, b_col // bc_w)),
            pl.BlockSpec((SSD_CHUNK, bc_w), lambda b, c: (b * nc + c, c_col // bc_w)),
            pl.BlockSpec((SSD_CHUNK, LANES), lambda b, c: (b * nc + c, 0)),
            _resident((SUBLANES, conv_dim), const),
            _resident((SUBLANES, conv_dim), const),
            _resident((SUBLANES, LANES), const),
            _resident((SUBLANES, LANES), const),
            _resident((SUBLANES, width), const),
            _resident((SUBLANES, width), const),
            _resident((LANES, width), const),
        ],
        out_specs=pl.BlockSpec((SSD_CHUNK, width), lambda b, c: (b * nc + c, 0)),
        out_shape=jax.ShapeDtypeStruct((t, width), BF16),
        scratch_shapes=[
            pltpu.VMEM((SUBLANES + SSD_CHUNK, conv_dim), F32),
            pltpu.VMEM((d_state, width), F32),
            pltpu.VMEM((SSD_CHUNK, conv_dim), F32),
            pltpu.VMEM((SSD_CHUNK, width), F32),
        ],
        compiler_params=_cparams(("parallel", "arbitrary")),
        name="ssd0",
    )(proj, proj, proj, proj, dt_raw, cw, cb, dtb, aneg, dsk, ng, expand)


def _rope(v, cos, sin):
    half = ROPE_DIM // 2
    lane = lax.broadcasted_iota(jnp.int32, v.shape, 1)
    partner = jnp.where(lane < half, pltpu.roll(v, LANES - half, 1), pltpu.roll(v, half, 1))
    return v * cos + partner * sin


def _moba_kernel(q_ref, k_ref, v_ref, g_ref, cos_ref, sin_ref, o_ref, kr_ref, vt_ref, km_ref, ch_ref,
                 sa_ref, sb_ref, *, n_blocks, heads_per_step):
    qb = pl.program_id(2)
    blk = MOBA_BLOCK
    hd = ATT_HEAD_DIM
    scale = hd ** -0.5 * math.log2(math.e)
    heads = range(heads_per_step)

    n_rows = km_ref.shape[1]

    @pl.when(qb == 0)
    def _():
        km_ref[...] = jnp.zeros_like(km_ref)
        for n in range(n_blocks):
            rows = slice(n * blk, (n + 1) * blk)
            for j in heads:
                cols = slice(j * hd, (j + 1) * hd)
                kr = _rope(k_ref[rows, cols].astype(F32), cos_ref[rows, :], sin_ref[rows, :])
                kr_ref[j, rows, :] = kr.astype(BF16)
                if n + 1 < n_rows:
                    km_ref[j, n + 1:n + 2, :] = jnp.mean(kr, axis=0, keepdims=True)
                vt_ref[j, 0:hd, rows] = v_ref[rows, cols].astype(F32).T.astype(BF16)
                vt_ref[j, hd:, rows] = jnp.ones((vt_ref.shape[1] - hd, blk), BF16)

    q0 = pl.multiple_of(qb * blk, blk)
    cos_q = cos_ref[pl.ds(q0, blk), :]
    sin_q = sin_ref[pl.ds(q0, blk), :]
    row_id = lax.broadcasted_iota(jnp.int32, (n_rows, blk), 0)
    past = (row_id >= 1) & (row_id <= qb)
    key_i = lax.broadcasted_iota(jnp.int32, (blk, blk), 0)
    qry_i = lax.broadcasted_iota(jnp.int32, (blk, blk), 1)

    qs_t = []
    for j in heads:
        qr_t = _rope(q_ref[:, j * hd:(j + 1) * hd].astype(F32), cos_q, sin_q).T
        qs_t.append((qr_t * scale).astype(BF16))

        s_own = jnp.dot(kr_ref[j, pl.ds(q0, blk), :], qs_t[j], preferred_element_type=F32)
        sa_ref[j] = jnp.where(key_i <= qry_i, s_own, NEG_INF)

        gate = jnp.where(past, _dot_hi(km_ref[j], qr_t), NEG_INF)
        rank = jnp.zeros((n_rows, blk), jnp.int32)
        for n2 in range(1, n_blocks):
            other = gate[n2:n2 + 1, :]
            rank = rank + jnp.where(other > gate, 1,
                                    jnp.where(other == gate, jnp.where(row_id > n2, 1, 0), 0))
        ch_ref[j] = jnp.where(row_id == 0, 1.0, jnp.where(past, jnp.where(rank < MOBA_TOPK, 1.0, 0.0), 0.0))

    def stage(k, ref):
        n0 = pl.multiple_of(jnp.clip(k - 1, 0, n_blocks - 1) * blk, blk)
        for j in heads:
            ref[j] = jnp.dot(kr_ref[j, pl.ds(n0, blk), :], qs_t[j], preferred_element_type=F32)

    def consume(k, ref, carry):
        n0 = pl.multiple_of(jnp.where(k == 0, qb, k - 1) * blk, blk)
        out = []
        for j in heads:
            m, acc = carry[j]
            taken = ch_ref[j, pl.ds(k, 1), :] > 0.5
            m_new = jnp.maximum(m, jnp.where(taken, jnp.max(ref[j], axis=0, keepdims=True), NEG_INF))
            alpha = jnp.exp2(m - m_new)
            p = jnp.exp2(ref[j] - jnp.where(taken, m_new, -NEG_INF)).astype(BF16)
            out.append((m_new, alpha * acc + jnp.dot(vt_ref[j, :, pl.ds(n0, blk)], p,
                                                     preferred_element_type=F32)))
        return tuple(out)

    def two_blocks(i, carry):
        stage(2 * i + 1, sb_ref)
        carry = consume(2 * i, sa_ref, carry)
        stage(2 * i + 2, sa_ref)
        return consume(2 * i + 1, sb_ref, carry)

    init = tuple((jnp.full((1, blk), NEG_INF, F32), jnp.zeros((vt_ref.shape[1], blk), F32)) for _ in heads)
    final = lax.fori_loop(0, qb // 2 + 1, two_blocks, init)
    for j in heads:
        _, acc = final[j]
        cols = slice(j * hd, (j + 1) * hd)
        out_t = acc[0:hd, :] / acc[hd:hd + 1, :]
        o_ref[:, cols] = (out_t.T * _silu(g_ref[:, cols].astype(F32))).astype(o_ref.dtype)


def _moba(proj, cos_t, sin_t, *, batch, seq, heads, q_col, k_col, v_col, g_col, heads_per_step):
    nb = seq // MOBA_BLOCK
    hd = ATT_HEAD_DIM
    hw = hd * heads_per_step
    t = batch * seq
    assert heads % heads_per_step == 0 and all(c % hw == 0 for c in (q_col, k_col, v_col, g_col))
    qc, kc, vc, gc = (c // hw for c in (q_col, k_col, v_col, g_col))
    kern = functools.partial(_moba_kernel, n_blocks=nb, heads_per_step=heads_per_step)
    return pl.pallas_call(
        kern,
        grid=(batch, heads // heads_per_step, nb),
        in_specs=[
            pl.BlockSpec((MOBA_BLOCK, hw), lambda b, h, i: (b * nb + i, qc + h)),
            pl.BlockSpec((seq, hw), lambda b, h, i: (b, kc + h)),
            pl.BlockSpec((seq, hw), lambda b, h, i: (b, vc + h)),
            pl.BlockSpec((MOBA_BLOCK, hw), lambda b, h, i: (b * nb + i, gc + h)),
            _resident((seq, hd), lambda b, h, i: (0, 0)),
            _resident((seq, hd), lambda b, h, i: (0, 0)),
        ],
        out_specs=pl.BlockSpec((MOBA_BLOCK, hw), lambda b, h, i: (b * nb + i, h)),
        out_shape=jax.ShapeDtypeStruct((t, heads * hd), BF16),
        scratch_shapes=[
            pltpu.VMEM((heads_per_step, seq, hd), BF16),
            pltpu.VMEM((heads_per_step, hd + 16, seq), BF16),
            pltpu.VMEM((heads_per_step, nb, hd), F32),
            pltpu.VMEM((heads_per_step, nb, MOBA_BLOCK), F32),
            pltpu.VMEM((heads_per_step, MOBA_BLOCK, MOBA_BLOCK), F32),
            pltpu.VMEM((heads_per_step, MOBA_BLOCK, MOBA_BLOCK), F32),
        ],
        compiler_params=_cparams(("parallel", "parallel", "arbitrary")),
        name="moba0",
    )(proj, proj, proj, proj, cos_t, sin_t)


def _out0_kernel(ya_ref, yb_ref, x_ref, w_ref, g_ref, b_ref, o_ref, *, alpha):
    ka = ya_ref.shape[1]
    h = jnp.dot(ya_ref[...], w_ref[0:ka, :], preferred_element_type=F32)
    h = h + jnp.dot(yb_ref[...], w_ref[ka:, :], preferred_element_type=F32)
    v = alpha * x_ref[...] + h
    mu = jnp.mean(v, axis=1, keepdims=True)
    vc = v - mu
    var = jnp.mean(vc * vc, axis=1, keepdims=True)
    o_ref[...] = vc * lax.rsqrt(var + LN_EPS) * g_ref[0:1, :] + b_ref[0:1, :]


def _out0(ya, yb, x2, w, g, b, *, alpha, tm):
    t, d = x2.shape
    ka, kb = ya.shape[1], yb.shape[1]
    return pl.pallas_call(
        functools.partial(_out0_kernel, alpha=alpha),
        grid=(t // tm,),
        in_specs=[
            pl.BlockSpec((tm, ka), lambda i: (i, 0)),
            pl.BlockSpec((tm, kb), lambda i: (i, 0)),
            pl.BlockSpec((tm, d), lambda i: (i, 0)),
            _resident((ka + kb, d), lambda i: (0, 0)),
            _resident((SUBLANES, d), lambda i: (0, 0)),
            _resident((SUBLANES, d), lambda i: (0, 0)),
        ],
        out_specs=pl.BlockSpec((tm, d), lambda i: (i, 0)),
        out_shape=jax.ShapeDtypeStruct((t, d), F32),
        compiler_params=_cparams(("parallel",)),
        name="out0",
    )(ya, yb, x2, w, g, b)


def _inproj1_kernel(x_ref, w_ref, o_ref):
    o_ref[0] = jnp.dot(w_ref[...], x_ref[0].astype(BF16), preferred_element_type=F32).astype(o_ref.dtype)


def _inproj1(xt, wt):
    l, d, r = xt.shape
    n = wt.shape[0]
    return pl.pallas_call(
        _inproj1_kernel,
        grid=(l,),
        in_specs=[pl.BlockSpec((1, d, r), lambda i: (i, 0, 0)), _resident((n, d), lambda i: (0, 0))],
        out_specs=pl.BlockSpec((1, n, r), lambda i: (i, 0, 0)),
        out_shape=jax.ShapeDtypeStruct((l, n, r), BF16),
        compiler_params=_cparams(("parallel",)),
        name="inproj1",
    )(xt, wt)


def _cpow(lr, li, d, nbits):
    pr = jnp.ones(d.shape, F32)
    pi = jnp.zeros(d.shape, F32)
    br, bi = lr, li
    for bit in range(nbits):
        on = ((d >> bit) & 1) == 1
        nr = pr * br - pi * bi
        ni = pr * bi + pi * br
        pr = jnp.where(on, nr, pr)
        pi = jnp.where(on, ni, pi)
        if bit + 1 < nbits:
            br, bi = br * br - bi * bi, 2.0 * br * bi
    return pr, pi


def _gelu_tanh(v):
    c = math.sqrt(2.0 / math.pi)
    h = 0.5 * v
    return h + h * jnp.tanh(v * (c + (c * 0.044715) * (v * v)))


def _s5_group(j, u_ref, lamc_ref, lamr_ref, bb_ref, rep_ref, cc_ref, ca_ref, cbm_ref, dsk_ref, o_ref, tz_ref,
              *, chunk, chunks_per_seq):
    m = S5_GROUP
    n = S5_STATE
    lm = chunk * m
    r = u_ref.shape[2]
    u = u_ref[:, j * m:(j + 1) * m, :].reshape(lm, r)

    pos_per_blk = LANES // m
    n_lane_blk = lm // LANES
    lam_r, lam_i = lamc_ref[j, 0], lamc_ref[j, 1]
    lane = lax.broadcasted_iota(jnp.int32, (n, LANES), 1)
    pr, pi = _cpow(lam_r, lam_i, (pos_per_blk - 1) - lane // m, (pos_per_blk - 1).bit_length())
    lbr, lbi = lam_r, lam_i
    for _ in range(pos_per_blk.bit_length() - 1):
        lbr, lbi = lbr * lbr - lbi * lbi, 2.0 * lbr * lbi
    bbr = _dot_split(bb_ref[j, 0], rep_ref[...], 3)
    bbi = _dot_split(bb_ref[j, 1], rep_ref[...], 3)
    bl_r, bl_i = [None] * n_lane_blk, [None] * n_lane_blk
    for c in range(n_lane_blk - 1, -1, -1):
        bl_r[c] = pr * bbr - pi * bbi
        bl_i[c] = pr * bbi + pi * bbr
        pr, pi = pr * lbr - pi * lbi, pr * lbi + pi * lbr
    bl = jnp.concatenate([jnp.concatenate(bl_r, axis=1), jnp.concatenate(bl_i, axis=1)], axis=0)

    krev = _dot_hi(cc_ref[j], bl)
    col = lax.broadcasted_iota(jnp.int32, (m, lm), 1)
    row = lax.broadcasted_iota(jnp.int32, (m, lm), 0)
    d_lanes = jnp.concatenate([dsk_ref[j]] * n_lane_blk, axis=1)
    krev = krev + jnp.where(col == (chunk - 1) * m + row, d_lanes, 0.0)

    col_s = col // m
    for t in range(chunk):
        rolled = pltpu.roll(krev, ((t + 1) * m) % lm, 1)
        tz_ref[j, t * m:(t + 1) * m, 0:lm] = jnp.where(col_s <= t, rolled, 0.0).astype(BF16)

    e = jnp.dot(bl.astype(BF16), u, preferred_element_type=F32)
    lane_r = lax.broadcasted_iota(jnp.int32, (n, r), 1) % chunks_per_seq
    hr = jnp.where(lane_r >= 1, pltpu.roll(e[:n], 1, 1), 0.0)
    hi = jnp.where(lane_r >= 1, pltpu.roll(e[n:], 1, 1), 0.0)
    mr, mi = lbr, lbi
    for _ in range((chunk // pos_per_blk).bit_length() - 1):
        mr, mi = mr * mr - mi * mi, 2.0 * mr * mi
    mr = jnp.concatenate([mr] * (r // LANES), axis=1) if r > LANES else mr[:, :r]
    mi = jnp.concatenate([mi] * (r // LANES), axis=1) if r > LANES else mi[:, :r]
    step = 1
    while step < chunks_per_seq:
        sr = jnp.where(lane_r >= step, pltpu.roll(hr, step, 1), 0.0)
        si = jnp.where(lane_r >= step, pltpu.roll(hi, step, 1), 0.0)
        hr, hi = hr + mr * sr - mi * si, hi + mr * si + mi * sr
        mr, mi = mr * mr - mi * mi, 2.0 * mr * mi
        step *= 2
    h_in = jnp.concatenate([hr, hi], axis=0).astype(BF16)

    lam_rr = jnp.broadcast_to(lamr_ref[j, 0:1, :], (m, 2 * n))
    lam_ri = jnp.broadcast_to(lamr_ref[j, 1:2, :], (m, 2 * n))
    ca, cbm = ca_ref[j], cbm_ref[j]
    qr, qi = lam_rr, lam_ri
    for t in range(chunk):
        tz_ref[j, t * m:(t + 1) * m, lm:] = (ca * qr + cbm * qi).astype(BF16)
        if t + 1 < chunk:
            qr, qi = qr * lam_rr - qi * lam_ri, qr * lam_ri + qi * lam_rr

    y = jnp.dot(tz_ref[j], jnp.concatenate([u, h_in], axis=0), preferred_element_type=F32)
    o_ref[:, j * m:(j + 1) * m, :] = _gelu_tanh(y).astype(o_ref.dtype).reshape(chunk, m, r)


def _s5_kernel(*refs, chunk, chunks_per_seq, groups_per_step):
    for j in range(groups_per_step):
        _s5_group(j, *refs, chunk=chunk, chunks_per_seq=chunks_per_seq)


def _s5(ug, lamc, lamr, bb, rep, cc, ca, cbm, dsk, *, chunks_per_seq):
    chunk, _, r = ug.shape
    g = lamc.shape[0]
    m, n = S5_GROUP, S5_STATE
    lm = chunk * m
    gps = 2
    kern = functools.partial(_s5_kernel, chunk=chunk, chunks_per_seq=chunks_per_seq, groups_per_step=gps)
    p4 = lambda i: (i, 0, 0, 0)
    p3 = lambda i: (i, 0, 0)
    return pl.pallas_call(
        kern,
        grid=(g // gps,),
        in_specs=[
            pl.BlockSpec((chunk, gps * m, r), lambda i: (0, i, 0)),
            pl.BlockSpec((gps, 2, n, LANES), p4),
            pl.BlockSpec((gps, SUBLANES, 2 * n), p3),
            pl.BlockSpec((gps, 2, n, m), p4),
            _resident((m, LANES), lambda i: (0, 0)),
            pl.BlockSpec((gps, m, 2 * n), p3),
            pl.BlockSpec((gps, m, 2 * n), p3),
            pl.BlockSpec((gps, m, 2 * n), p3),
            pl.BlockSpec((gps, m, LANES), p3),
        ],
        out_specs=pl.BlockSpec((chunk, gps * m, r), lambda i: (0, i, 0)),
        out_shape=jax.ShapeDtypeStruct((chunk, g * m, r), BF16),
        scratch_shapes=[pltpu.VMEM((gps, lm, lm + 2 * n), BF16)],
        compiler_params=_cparams(("parallel",)),
        name="s5scan",
    )(ug, lamc, lamr, bb, rep, cc, ca, cbm, dsk)


def _out1_kernel(y_ref, gate_ref, x_ref, wg_ref, wo_ref, g_ref, b_ref, o_ref, v_ref, s_ref, *, alpha):
    w = y_ref.shape[1]
    d = x_ref.shape[1]
    mc = 512
    y = y_ref[0]
    for c0 in range(0, w, mc):
        ga = jnp.dot(wg_ref[c0:c0 + mc, :], y, preferred_element_type=F32)
        gb = jnp.dot(wg_ref[w + c0:w + c0 + mc, :], y, preferred_element_type=F32)
        v_ref[c0:c0 + mc, :] = (ga * _sigmoid(gb) * _silu(gate_ref[0, c0:c0 + mc, :].astype(F32))).astype(BF16)
    v = v_ref[...]
    for c0 in range(0, d, mc):
        s_ref[c0:c0 + mc, :] = alpha * x_ref[0, c0:c0 + mc, :] + jnp.dot(
            wo_ref[c0:c0 + mc, :], v, preferred_element_type=F32)
    s = s_ref[...]
    mu = jnp.mean(s, axis=0, keepdims=True)
    sc = s - mu
    var = jnp.mean(sc * sc, axis=0, keepdims=True)
    o_ref[0] = sc * lax.rsqrt(var + LN_EPS) * g_ref[...] + b_ref[...]


def _out1(y3, ug, xt, wgt, wot, g_col, b_col, *, alpha):
    l, w, r = y3.shape
    d = xt.shape[1]
    gate_blk = 1
    return pl.pallas_call(
        functools.partial(_out1_kernel, alpha=alpha),
        grid=(l,),
        in_specs=[
            pl.BlockSpec((1, w, r), lambda i: (i, 0, 0)),
            pl.BlockSpec((1, w, r), lambda i: (i, gate_blk, 0)),
            pl.BlockSpec((1, d, r), lambda i: (i, 0, 0)),
            _resident((2 * w, w), lambda i: (0, 0)),
            _resident((d, w), lambda i: (0, 0)),
            _resident((d, 1), lambda i: (0, 0)),
            _resident((d, 1), lambda i: (0, 0)),
        ],
        out_specs=pl.BlockSpec((1, d, r), lambda i: (i, 0, 0)),
        out_shape=jax.ShapeDtypeStruct((l, d, r), F32),
        scratch_shapes=[pltpu.VMEM((w, r), BF16), pltpu.VMEM((d, r), F32)],
        compiler_params=_cparams(("parallel",)),
        name="out1",
    )(y3, ug, xt, wgt, wot, g_col, b_col)


def _pad_rows(v, rows=SUBLANES):
    v = jnp.atleast_2d(v.astype(F32))
    return jnp.pad(v, ((0, rows - v.shape[0]), (0, 0)))


def _pad_lanes(v, lanes=LANES):
    return jnp.pad(v, [(0, 0)] * (v.ndim - 1) + [(0, lanes - v.shape[-1])])


def _layer0(x2, batch, seq, in_w, conv_w, conv_b, dt_bias, a_log, d_skip, norm_g, out_w, ln_g, ln_b, alpha):
    d = x2.shape[1]
    ssd_heads = dt_bias.shape[0]
    width = ssd_heads * SSD_HEAD_DIM
    conv_dim = conv_w.shape[1]
    d_state = (conv_dim - width) // (2 * SSD_GROUPS)
    att_width = (in_w.shape[1] - width - conv_dim - ssd_heads) // 4
    att_heads = att_width // ATT_HEAD_DIM

    z_end = width
    xbc_end = z_end + conv_dim
    dt_end = xbc_end + ssd_heads
    w_t = in_w.T
    w_main = jnp.concatenate([w_t[:xbc_end], w_t[dt_end:]], axis=0).astype(BF16)
    w_dt = jnp.pad(w_t[xbc_end:dt_end], ((0, LANES - ssd_heads), (0, 0))).astype(BF16)
    z_col, xbc_col = 0, z_end
    q_col = xbc_end
    k_col, v_col, g_col = q_col + att_width, q_col + 2 * att_width, q_col + 3 * att_width

    proj, dt_raw = _inproj(x2, w_main, w_dt, tm=1024, tn=1024)

    expand = (jnp.arange(LANES)[:, None] == (jnp.arange(width)[None, :] // SSD_HEAD_DIM)).astype(BF16)
    ya = _ssd(proj, dt_raw, _pad_rows(conv_w), _pad_rows(conv_b), _pad_rows(_pad_lanes(dt_bias.astype(F32))),
              _pad_rows(_pad_lanes(-jnp.exp(a_log.astype(F32)))),
              _pad_rows(jnp.repeat(d_skip.astype(F32), SSD_HEAD_DIM)), _pad_rows(norm_g), expand,
              batch=batch, seq=seq, width=width, n_groups=SSD_GROUPS, d_state=d_state,
              z_col=z_col, xbc_col=xbc_col)

    half = ROPE_DIM // 2
    inv_freq = ROPE_THETA ** (-(jnp.arange(half, dtype=F32) * 2.0 / ROPE_DIM))
    ang = jnp.arange(seq, dtype=F32)[:, None] * inv_freq[None, :]
    ones = jnp.ones((seq, ATT_HEAD_DIM - ROPE_DIM), F32)
    cos_t = jnp.concatenate([jnp.cos(ang), jnp.cos(ang), ones], axis=1)
    sin_t = jnp.concatenate([-jnp.sin(ang), jnp.sin(ang), 0.0 * ones], axis=1)
    yb = _moba(proj, cos_t, sin_t, batch=batch, seq=seq, heads=att_heads,
               q_col=q_col, k_col=k_col, v_col=v_col, g_col=g_col, heads_per_step=4)

    return _out0(ya, yb, x2, out_w.astype(BF16), _pad_rows(ln_g), _pad_rows(ln_b), alpha=alpha, tm=512)


def _layer1(x2, batch, seq, in_w, lam_re, lam_im, log_dt, b_re, b_im, c_re, c_im, d_skip, glu_w, out_w,
            ln_g, ln_b, alpha):
    t, d = x2.shape
    chunk = S5_L
    cps = seq // chunk
    r = batch * cps
    groups, n = lam_re.shape
    m = S5_GROUP
    w = groups * m

    lre, lim = lam_re.astype(F32), lam_im.astype(F32)
    dt = jnp.exp(log_dt.astype(F32))[:, None]
    mag = jnp.exp(lre * dt)
    lbr, lbi = mag * jnp.cos(lim * dt), mag * jnp.sin(lim * dt)
    den = lre * lre + lim * lim
    fr = ((lbr - 1.0) * lre + lbi * lim) / den
    fi = (lbi * lre - (lbr - 1.0) * lim) / den
    bre, bim = b_re.astype(F32), b_im.astype(F32)
    bbr = fr[..., None] * bre - fi[..., None] * bim
    bbi = fr[..., None] * bim + fi[..., None] * bre
    lamc = jnp.broadcast_to(jnp.stack([lbr, lbi], axis=1)[..., None], (groups, 2, n, LANES))
    lamr = jnp.stack([jnp.concatenate([lbr, lbr], -1), jnp.concatenate([lbi, lbi], -1)], axis=1)
    lamr = jnp.pad(lamr, ((0, 0), (0, SUBLANES - 2), (0, 0)))
    bb = jnp.stack([bbr, bbi], axis=1)
    rep = (jnp.arange(m)[:, None] == (jnp.arange(LANES)[None, :] % m)).astype(BF16)
    cr, ci = c_re.astype(F32), c_im.astype(F32)
    cc = jnp.concatenate([cr, -ci], axis=-1)
    cbm = jnp.concatenate([-ci, -cr], axis=-1)
    dsk = jnp.broadcast_to(d_skip.astype(F32).reshape(groups, m, 1), (groups, m, LANES))

    xt = x2.reshape(r, chunk, d).transpose(1, 2, 0)
    ug = _inproj1(xt, in_w.T.astype(BF16))
    y3 = _s5(ug, lamc, lamr, bb, rep, cc, cc, cbm, dsk, chunks_per_seq=cps)
    o3 = _out1(y3, ug, xt, glu_w.T.astype(BF16), out_w.T.astype(BF16),
               ln_g.astype(F32)[:, None], ln_b.astype(F32)[:, None], alpha=alpha)
    return o3.transpose(2, 0, 1).reshape(t, d)


def kernel(x, in0_w, conv_w, conv_b, dt_bias, a_log, ssd_d, ssd_norm_g, out0_w, in1_w, s5_lam_re, s5_lam_im,
           s5_log_dt, s5_b_re, s5_b_im, s5_c_re, s5_c_im, s5_d, glu_w, out1_w, ln_g, ln_b):
    batch, seq, d = x.shape
    depth = ln_g.shape[0]
    alpha = (2 * depth) ** 0.25
    x2 = x.reshape(batch * seq, d)
    for layer in range(depth):
        i = layer // 2
        if layer % 2 == 0:
            x2 = _layer0(x2, batch, seq, in0_w[i], conv_w[i], conv_b[i], dt_bias[i], a_log[i], ssd_d[i],
                         ssd_norm_g[i], out0_w[i], ln_g[layer], ln_b[layer], alpha)
        else:
            x2 = _layer1(x2, batch, seq, in1_w[i], s5_lam_re[i], s5_lam_im[i], s5_log_dt[i], s5_b_re[i],
                         s5_b_im[i], s5_c_re[i], s5_c_im[i], s5_d[i], glu_w[i], out1_w[i],
                         ln_g[layer], ln_b[layer], alpha)
    return x2.reshape(batch, seq, d).astype(x.dtype)
```

```python
import functools
import math

import jax
import jax.numpy as jnp
from jax import lax
from jax.experimental import pallas as pl
from jax.experimental.pallas import tpu as pltpu

F32 = jnp.float32
BF16 = jnp.bfloat16

SSD_HEAD_DIM = 64
SSD_GROUPS = 4
SSD_STATE = 128
SSD_CONV = 4
SSD_CHUNK = 256
ATT_HEAD_DIM = 128
MOBA_BLOCK = 256
MOBA_TOPK = 3
ROPE_THETA = 500000.0
ROPE_DIM = ATT_HEAD_DIM // 4
S5_GROUP = 16
S5_STATE = 64
LN_EPS = 1e-5
RMS_EPS = 1e-5
NEG_INF = -1e30

LANES = 128
SUBLANES = 8
VMEM_LIMIT = 56 * 1024 * 1024

S5_L = 64


def _cparams(sem):
    return pltpu.CompilerParams(dimension_semantics=sem, vmem_limit_bytes=VMEM_LIMIT)


def _resident(shape, index_map):
    return pl.BlockSpec(shape, index_map, pipeline_mode=pl.Buffered(1))


def _sigmoid(v):
    return 0.5 + 0.5 * jnp.tanh(0.5 * v)


def _silu(v):
    h = 0.5 * v
    return h + h * jnp.tanh(h)


def _nt_dot(a, b):
    return lax.dot_general(a, b, (((1,), (1,)), ((), ())), preferred_element_type=F32)


def _tn_dot(a, b):
    return lax.dot_general(a, b, (((0,), (0,)), ((), ())), preferred_element_type=F32)


def _dot_hi(a, b):
    a_hi = a.astype(BF16)
    b_hi = b.astype(BF16)
    a_lo = (a - a_hi.astype(F32)).astype(BF16)
    b_lo = (b - b_hi.astype(F32)).astype(BF16)
    return (jnp.dot(a_hi, b_hi, preferred_element_type=F32) + jnp.dot(a_hi, b_lo, preferred_element_type=F32)
            + jnp.dot(a_lo, b_hi, preferred_element_type=F32))


def _dot_split(a, b, passes, split_lhs=True):
    rem = a if split_lhs else b
    acc = None
    for _ in range(passes):
        piece = rem.astype(BF16)
        part = jnp.dot(piece, b, preferred_element_type=F32) if split_lhs else jnp.dot(
            a, piece, preferred_element_type=F32)
        acc = part if acc is None else acc + part
        rem = rem - piece.astype(F32)
    return acc


def _inproj_kernel(x_ref, w_ref, wdt_ref, o_ref, dt_ref, xb_ref):
    @pl.when(pl.program_id(1) == 0)
    def _():
        xb = x_ref[...].astype(BF16)
        xb_ref[...] = xb
        dt_ref[...] = _nt_dot(xb, wdt_ref[...])

    o_ref[...] = _nt_dot(xb_ref[...], w_ref[...]).astype(o_ref.dtype)


def _inproj(x2, w_t, w_dt_t, tm, tn, gap_at, gap):
    t, d = x2.shape
    n = w_t.shape[0] - gap
    assert gap_at % tn == 0 and n % tn == 0
    return pl.pallas_call(
        _inproj_kernel,
        grid=(t // tm, n // tn),
        in_specs=[
            pl.BlockSpec((tm, d), lambda i, j: (i, 0)),
            pl.BlockSpec((pl.Element(tn), pl.Element(d)),
                         lambda i, j: (pl.multiple_of(j * tn + jnp.where(j * tn >= gap_at, gap, 0),
                                                      math.gcd(tn, gap)), 0)),
            _resident((LANES, d), lambda i, j: (0, 0)),
        ],
        out_specs=[
            pl.BlockSpec((tm, tn), lambda i, j: (i, j)),
            pl.BlockSpec((tm, LANES), lambda i, j: (i, 0)),
        ],
        out_shape=[jax.ShapeDtypeStruct((t, n), BF16), jax.ShapeDtypeStruct((t, LANES), F32)],
        scratch_shapes=[pltpu.VMEM((tm, d), BF16)],
        compiler_params=_cparams(("parallel", "arbitrary")),
        name="inproj0",
    )(x2, w_t, w_dt_t)


def _ssd_kernel(z_ref, xs_ref, bm_ref, cm_ref, dt_ref, cw_ref, cb_ref, dtb_ref, aneg_ref, dsk_ref, ng_ref,
                exp_ref, o_ref, xpad_ref, state_ref, act_ref, y_ref, *, n_groups, head_dim):
    c = pl.program_id(1)
    chunk, width = z_ref.shape
    d_state = bm_ref.shape[1] // n_groups
    heads_per_group = width // head_dim // n_groups
    gw = heads_per_group * head_dim

    @pl.when(c == 0)
    def _():
        xpad_ref[0:SUBLANES, :] = jnp.zeros((SUBLANES, xpad_ref.shape[1]), F32)
        state_ref[...] = jnp.zeros_like(state_ref)

    strip = 512
    col0 = 0
    for src in (xs_ref, bm_ref, cm_ref):
        for s0 in range(0, src.shape[1], strip):
            cols = slice(col0 + s0, col0 + s0 + strip)
            xs_ = src[:, s0:s0 + strip].astype(F32)
            xpad_ref[SUBLANES:, cols] = xs_
            acc = cb_ref[0:1, cols] + cw_ref[SSD_CONV - 1:SSD_CONV, cols] * xs_
            for k in range(1, SSD_CONV):
                acc = acc + cw_ref[SSD_CONV - 1 - k:SSD_CONV - k, cols] * xpad_ref[SUBLANES - k:SUBLANES - k + chunk, cols]
            act_ref[:, cols] = _silu(acc)
            xpad_ref[0:SUBLANES, cols] = xs_[chunk - SUBLANES:, :]
        col0 += src.shape[1]

    dt_in = dt_ref[...] + dtb_ref[0:1, :]
    dt = jnp.maximum(dt_in, 0.0) + jnp.log(1.0 + jnp.exp(-jnp.abs(dt_in)))
    a = dt * aneg_ref[0:1, :]
    row = lax.broadcasted_iota(jnp.int32, (chunk, chunk), 0)
    col = lax.broadcasted_iota(jnp.int32, (chunk, chunk), 1)
    causal = row >= col
    cs = _dot_split(jnp.where(causal, 1.0, 0.0).astype(BF16), a, 3, split_lhs=False)
    cs2 = cs * math.log2(math.e)
    cs2_t = cs2.T
    cs_last = cs[chunk - 1:chunk, :]
    expand = exp_ref[...]
    dt_x = _dot_split(dt, expand, 2)
    ecs_x = _dot_split(jnp.exp(cs), expand, 2)
    dec_x = _dot_split(jnp.exp(cs_last - cs), expand, 2)
    sdec_x = _dot_split(jnp.broadcast_to(jnp.exp(cs_last), (SUBLANES, LANES)), expand, 2)[0:1, :]

    lane = lax.broadcasted_iota(jnp.int32, (chunk, LANES), 1)
    first_half = lane < head_dim
    for g in range(n_groups):
        gs = slice(g * gw, (g + 1) * gw)
        b_g = act_ref[:, width + g * d_state:width + (g + 1) * d_state].astype(BF16)
        c_g = act_ref[:, width + (n_groups + g) * d_state:width + (n_groups + g + 1) * d_state].astype(BF16)
        xs_g = act_ref[:, gs]
        xdt = xs_g * dt_x[:, gs]
        xdt_b = xdt.astype(BF16)
        cb = jnp.where(causal, _nt_dot(c_g, b_g), 0.0)
        st = state_ref[:, gs]
        y_g = jnp.dot(c_g, st.astype(BF16), preferred_element_type=F32) * ecs_x[:, gs] + dsk_ref[0:1, gs] * xs_g
        state_ref[:, gs] = st * sdec_x[:, gs] + _tn_dot(b_g, (xdt * dec_x[:, gs]).astype(BF16))
        for pr in range(heads_per_group // 2):
            xp = xdt_b[:, pr * LANES:(pr + 1) * LANES]
            halves = []
            for half in range(2):
                h = g * heads_per_group + 2 * pr + half
                seg = cs2[:, h:h + 1] - cs2_t[h:h + 1, :]
                m = (cb * jnp.exp2(jnp.minimum(seg, 0.0))).astype(BF16)
                halves.append(jnp.dot(m, xp, preferred_element_type=F32))
            y_pair = jnp.where(first_half, halves[0], halves[1])
            y_ref[:, g * gw + pr * LANES:g * gw + (pr + 1) * LANES] = y_pair + y_g[:, pr * LANES:(pr + 1) * LANES]

    yz = y_ref[...] * _silu(z_ref[...].astype(F32))
    ms = jnp.mean(yz * yz, axis=1, keepdims=True)
    o_ref[...] = (yz * lax.rsqrt(ms + RMS_EPS) * ng_ref[0:1, :]).astype(o_ref.dtype)


def _ssd(proj, dt_raw, cw, cb, dtb, aneg, dsk, ng, expand, *, batch, seq, width, n_groups, d_state,
         z_col, xbc_col):
    conv_dim = width + 2 * n_groups * d_state
    bc_w = n_groups * d_state
    nc = seq // SSD_CHUNK
    t = batch * seq
    kern = functools.partial(_ssd_kernel, n_groups=n_groups, head_dim=SSD_HEAD_DIM)
    b_col, c_col = xbc_col + width, xbc_col + width + bc_w
    assert z_col % width == 0 and xbc_col % width == 0 and b_col % bc_w == 0 and c_col % bc_w == 0
    const = lambda b, c: (0, 0)
    return pl.pallas_call(
        kern,
        grid=(batch, nc),
        in_specs=[
            pl.BlockSpec((SSD_CHUNK, width), lambda b, c: (b * nc + c, z_col // width)),
            pl.BlockSpec((SSD_CHUNK, width), lambda b, c: (b * nc + c, xbc_col // width)),
            pl.BlockSpec((SSD_CHUNK, bc_w), lambda b, c: (b * nc + c, b_col // bc_w)),
            pl.BlockSpec((SSD_CHUNK, bc_w), lambda b, c: (b * nc + c, c_col // bc_w)),
            pl.BlockSpec((SSD_CHUNK, LANES), lambda b, c: (b * nc + c, 0)),
            _resident((SUBLANES, conv_dim), const),
            _resident((SUBLANES, conv_dim), const),
            _resident((SUBLANES, LANES), const),
            _resident((SUBLANES, LANES), const),
            _resident((SUBLANES, width), const),
            _resident((SUBLANES, width), const),
            _resident((LANES, width), const),
        ],
        out_specs=pl.BlockSpec((SSD_CHUNK, width), lambda b, c: (b * nc + c, 0)),
        out_shape=jax.ShapeDtypeStruct((t, width), BF16),
        scratch_shapes=[
            pltpu.VMEM((SUBLANES + SSD_CHUNK, conv_dim), F32),
            pltpu.VMEM((d_state, width), F32),
            pltpu.VMEM((SSD_CHUNK, conv_dim), F32),
            pltpu.VMEM((SSD_CHUNK, width), F32),
        ],
        compiler_params=_cparams(("parallel", "arbitrary")),
        name="ssd0",
    )(proj, proj, proj, proj, dt_raw, cw, cb, dtb, aneg, dsk, ng, expand)


def _rope(v, cos, sin):
    half = ROPE_DIM // 2
    lane = lax.broadcasted_iota(jnp.int32, v.shape, 1)
    partner = jnp.where(lane < half, pltpu.roll(v, LANES - half, 1), pltpu.roll(v, half, 1))
    return v * cos + partner * sin


def _moba_kernel(q_ref, k_ref, v_ref, g_ref, cos_ref, sin_ref, o_ref, kr_ref, vt_ref, km_ref, ch_ref,
                 sa_ref, sb_ref, *, n_blocks, heads_per_step):
    qb = pl.program_id(2)
    blk = MOBA_BLOCK
    hd = ATT_HEAD_DIM
    scale = hd ** -0.5 * math.log2(math.e)
    heads = range(heads_per_step)

    n_rows = km_ref.shape[1]

    @pl.when(qb == 0)
    def _():
        km_ref[...] = jnp.zeros_like(km_ref)
        for n in range(n_blocks):
            rows = slice(n * blk, (n + 1) * blk)
            for j in heads:
                cols = slice(j * hd, (j + 1) * hd)
                kr = _rope(k_ref[rows, cols].astype(F32), cos_ref[rows, :], sin_ref[rows, :])
                kr_ref[j, rows, :] = kr.astype(BF16)
                if n + 1 < n_rows:
                    km_ref[j, n + 1:n + 2, :] = jnp.mean(kr, axis=0, keepdims=True)
                vt_ref[j, 0:hd, rows] = v_ref[rows, cols].astype(F32).T.astype(BF16)
                vt_ref[j, hd:, rows] = jnp.ones((vt_ref.shape[1] - hd, blk), BF16)

    q0 = pl.multiple_of(qb * blk, blk)
    cos_q = cos_ref[pl.ds(q0, blk), :]
    sin_q = sin_ref[pl.ds(q0, blk), :]
    row_id = lax.broadcasted_iota(jnp.int32, (n_rows, blk), 0)
    past = (row_id >= 1) & (row_id <= qb)
    key_i = lax.broadcasted_iota(jnp.int32, (blk, blk), 0)
    qry_i = lax.broadcasted_iota(jnp.int32, (blk, blk), 1)

    qs_t = []
    for j in heads:
        qr_t = _rope(q_ref[:, j * hd:(j + 1) * hd].astype(F32), cos_q, sin_q).T
        qs_t.append((qr_t * scale).astype(BF16))

        s_own = jnp.dot(kr_ref[j, pl.ds(q0, blk), :], qs_t[j], preferred_element_type=F32)
        sa_ref[j] = jnp.where(key_i <= qry_i, s_own, NEG_INF)

        gate = jnp.where(past, _dot_hi(km_ref[j], qr_t), NEG_INF)
        rank = jnp.zeros((n_rows, blk), jnp.int32)
        for n2 in range(1, n_blocks):
            other = gate[n2:n2 + 1, :]
            rank = rank + jnp.where(other > gate, 1,
                                    jnp.where(other == gate, jnp.where(row_id > n2, 1, 0), 0))
        ch_ref[j] = jnp.where(row_id == 0, 1.0, jnp.where(past, jnp.where(rank < MOBA_TOPK, 1.0, 0.0), 0.0))

    def stage(k, ref):
        n0 = pl.multiple_of(jnp.clip(k - 1, 0, n_blocks - 1) * blk, blk)
        for j in heads:
            ref[j] = jnp.dot(kr_ref[j, pl.ds(n0, blk), :], qs_t[j], preferred_element_type=F32)

    def consume(k, ref, carry):
        n0 = pl.multiple_of(jnp.where(k == 0, qb, k - 1) * blk, blk)
        out = []
        for j in heads:
            m, acc = carry[j]
            taken = ch_ref[j, pl.ds(k, 1), :] > 0.5
            m_new = jnp.maximum(m, jnp.where(taken, jnp.max(ref[j], axis=0, keepdims=True), NEG_INF))
            alpha = jnp.exp2(m - m_new)
            p = jnp.exp2(ref[j] - jnp.where(taken, m_new, -NEG_INF)).astype(BF16)
            out.append((m_new, alpha * acc + jnp.dot(vt_ref[j, :, pl.ds(n0, blk)], p,
                                                     preferred_element_type=F32)))
        return tuple(out)

    def two_blocks(i, carry):
        stage(2 * i + 1, sb_ref)
        carry = consume(2 * i, sa_ref, carry)
        stage(2 * i + 2, sa_ref)
        return consume(2 * i + 1, sb_ref, carry)

    init = tuple((jnp.full((1, blk), NEG_INF, F32), jnp.zeros((vt_ref.shape[1], blk), F32)) for _ in heads)
    final = lax.fori_loop(0, qb // 2 + 1, two_blocks, init)
    for j in heads:
        _, acc = final[j]
        cols = slice(j * hd, (j + 1) * hd)
        out_t = acc[0:hd, :] / acc[hd:hd + 1, :]
        o_ref[:, cols] = (out_t.T * _silu(g_ref[:, cols].astype(F32))).astype(o_ref.dtype)


def _moba(proj, cos_t, sin_t, *, batch, seq, heads, q_col, k_col, v_col, g_col, heads_per_step):
    nb = seq // MOBA_BLOCK
    hd = ATT_HEAD_DIM
    hw = hd * heads_per_step
    t = batch * seq
    assert heads % heads_per_step == 0 and all(c % hw == 0 for c in (q_col, k_col, v_col, g_col))
    qc, kc, vc, gc = (c // hw for c in (q_col, k_col, v_col, g_col))
    kern = functools.partial(_moba_kernel, n_blocks=nb, heads_per_step=heads_per_step)
    return pl.pallas_call(
        kern,
        grid=(batch, heads // heads_per_step, nb),
        in_specs=[
            pl.BlockSpec((MOBA_BLOCK, hw), lambda b, h, i: (b * nb + i, qc + h)),
            pl.BlockSpec((seq, hw), lambda b, h, i: (b, kc + h)),
            pl.BlockSpec((seq, hw), lambda b, h, i: (b, vc + h)),
            pl.BlockSpec((MOBA_BLOCK, hw), lambda b, h, i: (b * nb + i, gc + h)),
            _resident((seq, hd), lambda b, h, i: (0, 0)),
            _resident((seq, hd), lambda b, h, i: (0, 0)),
        ],
        out_specs=pl.BlockSpec((MOBA_BLOCK, hw), lambda b, h, i: (b * nb + i, h)),
        out_shape=jax.ShapeDtypeStruct((t, heads * hd), BF16),
        scratch_shapes=[
            pltpu.VMEM((heads_per_step, seq, hd), BF16),
            pltpu.VMEM((heads_per_step, hd + 16, seq), BF16),
            pltpu.VMEM((heads_per_step, nb, hd), F32),
            pltpu.VMEM((heads_per_step, nb, MOBA_BLOCK), F32),
            pltpu.VMEM((heads_per_step, MOBA_BLOCK, MOBA_BLOCK), F32),
            pltpu.VMEM((heads_per_step, MOBA_BLOCK, MOBA_BLOCK), F32),
        ],
        compiler_params=_cparams(("parallel", "parallel", "arbitrary")),
        name="moba0",
    )(proj, proj, proj, proj, cos_t, sin_t)


def _out0_kernel(ya_ref, yb_ref, x_ref, w_ref, g_ref, b_ref, o_ref, *, alpha):
    ka = ya_ref.shape[1]
    h = jnp.dot(ya_ref[...], w_ref[0:ka, :], preferred_element_type=F32)
    h = h + jnp.dot(yb_ref[...], w_ref[ka:, :], preferred_element_type=F32)
    v = alpha * x_ref[...] + h
    mu = jnp.mean(v, axis=1, keepdims=True)
    vc = v - mu
    var = jnp.mean(vc * vc, axis=1, keepdims=True)
    o_ref[...] = vc * lax.rsqrt(var + LN_EPS) * g_ref[0:1, :] + b_ref[0:1, :]


def _out0(ya, yb, x2, w, g, b, *, alpha, tm):
    t, d = x2.shape
    ka, kb = ya.shape[1], yb.shape[1]
    return pl.pallas_call(
        functools.partial(_out0_kernel, alpha=alpha),
        grid=(t // tm,),
        in_specs=[
            pl.BlockSpec((tm, ka), lambda i: (i, 0)),
            pl.BlockSpec((tm, kb), lambda i: (i, 0)),
            pl.BlockSpec((tm, d), lambda i: (i, 0)),
            _resident((ka + kb, d), lambda i: (0, 0)),
            _resident((SUBLANES, d), lambda i: (0, 0)),
            _resident((SUBLANES, d), lambda i: (0, 0)),
        ],
        out_specs=pl.BlockSpec((tm, d), lambda i: (i, 0)),
        out_shape=jax.ShapeDtypeStruct((t, d), F32),
        compiler_params=_cparams(("parallel",)),
        name="out0",
    )(ya, yb, x2, w, g, b)


def _inproj1_kernel(x_ref, w_ref, o_ref):
    o_ref[0] = jnp.dot(w_ref[...], x_ref[0].astype(BF16), preferred_element_type=F32).astype(o_ref.dtype)


def _inproj1(xt, wt):
    l, d, r = xt.shape
    n = wt.shape[0]
    return pl.pallas_call(
        _inproj1_kernel,
        grid=(l,),
        in_specs=[pl.BlockSpec((1, d, r), lambda i: (i, 0, 0)), _resident((n, d), lambda i: (0, 0))],
        out_specs=pl.BlockSpec((1, n, r), lambda i: (i, 0, 0)),
        out_shape=jax.ShapeDtypeStruct((l, n, r), BF16),
        compiler_params=_cparams(("parallel",)),
        name="inproj1",
    )(xt, wt)


def _cpow(lr, li, d, nbits):
    pr = jnp.ones(d.shape, F32)
    pi = jnp.zeros(d.shape, F32)
    br, bi = lr, li
    for bit in range(nbits):
        on = ((d >> bit) & 1) == 1
        nr = pr * br - pi * bi
        ni = pr * bi + pi * br
        pr = jnp.where(on, nr, pr)
        pi = jnp.where(on, ni, pi)
        if bit + 1 < nbits:
            br, bi = br * br - bi * bi, 2.0 * br * bi
    return pr, pi


def _gelu_tanh(v):
    c = math.sqrt(2.0 / math.pi)
    h = 0.5 * v
    return h + h * jnp.tanh(v * (c + (c * 0.044715) * (v * v)))


def _s5_group(j, u_ref, lamc_ref, lamr_ref, bb_ref, rep_ref, cc_ref, ca_ref, cbm_ref, dsk_ref, o_ref, tz_ref,
              *, chunk, chunks_per_seq):
    m = S5_GROUP
    n = S5_STATE
    lm = chunk * m
    r = u_ref.shape[2]
    u = u_ref[:, j * m:(j + 1) * m, :].reshape(lm, r)

    pos_per_blk = LANES // m
    n_lane_blk = lm // LANES
    lam_r, lam_i = lamc_ref[j, 0], lamc_ref[j, 1]
    lane = lax.broadcasted_iota(jnp.int32, (n, LANES), 1)
    pr, pi = _cpow(lam_r, lam_i, (pos_per_blk - 1) - lane // m, (pos_per_blk - 1).bit_length())
    lbr, lbi = lam_r, lam_i
    for _ in range(pos_per_blk.bit_length() - 1):
        lbr, lbi = lbr * lbr - lbi * lbi, 2.0 * lbr * lbi
    bbr = _dot_split(bb_ref[j, 0], rep_ref[...], 3)
    bbi = _dot_split(bb_ref[j, 1], rep_ref[...], 3)
    bl_r, bl_i = [None] * n_lane_blk, [None] * n_lane_blk
    for c in range(n_lane_blk - 1, -1, -1):
        bl_r[c] = pr * bbr - pi * bbi
        bl_i[c] = pr * bbi + pi * bbr
        pr, pi = pr * lbr - pi * lbi, pr * lbi + pi * lbr
    bl = jnp.concatenate([jnp.concatenate(bl_r, axis=1), jnp.concatenate(bl_i, axis=1)], axis=0)

    krev = _dot_hi(cc_ref[j], bl)
    col = lax.broadcasted_iota(jnp.int32, (m, lm), 1)
    row = lax.broadcasted_iota(jnp.int32, (m, lm), 0)
    d_lanes = jnp.concatenate([dsk_ref[j]] * n_lane_blk, axis=1)
    krev = krev + jnp.where(col == (chunk - 1) * m + row, d_lanes, 0.0)

    col_s = col // m
    for t in range(chunk):
        rolled = pltpu.roll(krev, ((t + 1) * m) % lm, 1)
        tz_ref[j, t * m:(t + 1) * m, 0:lm] = jnp.where(col_s <= t, rolled, 0.0).astype(BF16)

    e = jnp.dot(bl.astype(BF16), u, preferred_element_type=F32)
    lane_r = lax.broadcasted_iota(jnp.int32, (n, r), 1) % chunks_per_seq
    hr = jnp.where(lane_r >= 1, pltpu.roll(e[:n], 1, 1), 0.0)
    hi = jnp.where(lane_r >= 1, pltpu.roll(e[n:], 1, 1), 0.0)
    mr, mi = lbr, lbi
    for _ in range((chunk // pos_per_blk).bit_length() - 1):
        mr, mi = mr * mr - mi * mi, 2.0 * mr * mi
    mr = jnp.concatenate([mr] * (r // LANES), axis=1) if r > LANES else mr[:, :r]
    mi = jnp.concatenate([mi] * (r // LANES), axis=1) if r > LANES else mi[:, :r]
    step = 1
    while step < chunks_per_seq:
        sr = jnp.where(lane_r >= step, pltpu.roll(hr, step, 1), 0.0)
        si = jnp.where(lane_r >= step, pltpu.roll(hi, step, 1), 0.0)
        hr, hi = hr + mr * sr - mi * si, hi + mr * si + mi * sr
        mr, mi = mr * mr - mi * mi, 2.0 * mr * mi
        step *= 2
    h_in = jnp.concatenate([hr, hi], axis=0).astype(BF16)

    lam_rr = jnp.broadcast_to(lamr_ref[j, 0:1, :], (m, 2 * n))
    lam_ri = jnp.broadcast_to(lamr_ref[j, 1:2, :], (m, 2 * n))
    ca, cbm = ca_ref[j], cbm_ref[j]
    qr, qi = lam_rr, lam_ri
    for t in range(chunk):
        tz_ref[j, t * m:(t + 1) * m, lm:] = (ca * qr + cbm * qi).astype(BF16)
        if t + 1 < chunk:
            qr, qi = qr * lam_rr - qi * lam_ri, qr * lam_ri + qi * lam_rr

    y = jnp.dot(tz_ref[j], jnp.concatenate([u, h_in], axis=0), preferred_element_type=F32)
    o_ref[:, j * m:(j + 1) * m, :] = _gelu_tanh(y).astype(o_ref.dtype).reshape(chunk, m, r)


def _s5_kernel(*refs, chunk, chunks_per_seq, groups_per_step):
    for j in range(groups_per_step):
        _s5_group(j, *refs, chunk=chunk, chunks_per_seq=chunks_per_seq)


def _s5(ug, lamc, lamr, bb, rep, cc, ca, cbm, dsk, *, chunks_per_seq):
    chunk, _, r = ug.shape
    g = lamc.shape[0]
    m, n = S5_GROUP, S5_STATE
    lm = chunk * m
    gps = 2
    kern = functools.partial(_s5_kernel, chunk=chunk, chunks_per_seq=chunks_per_seq, groups_per_step=gps)
    p4 = lambda i: (i, 0, 0, 0)
    p3 = lambda i: (i, 0, 0)
    return pl.pallas_call(
        kern,
        grid=(g // gps,),
        in_specs=[
            pl.BlockSpec((chunk, gps * m, r), lambda i: (0, i, 0)),
            pl.BlockSpec((gps, 2, n, LANES), p4),
            pl.BlockSpec((gps, SUBLANES, 2 * n), p3),
            pl.BlockSpec((gps, 2, n, m), p4),
            _resident((m, LANES), lambda i: (0, 0)),
            pl.BlockSpec((gps, m, 2 * n), p3),
            pl.BlockSpec((gps, m, 2 * n), p3),
            pl.BlockSpec((gps, m, 2 * n), p3),
            pl.BlockSpec((gps, m, LANES), p3),
        ],
        out_specs=pl.BlockSpec((chunk, gps * m, r), lambda i: (0, i, 0)),
        out_shape=jax.ShapeDtypeStruct((chunk, g * m, r), BF16),
        scratch_shapes=[pltpu.VMEM((gps, lm, lm + 2 * n), BF16)],
        compiler_params=_cparams(("parallel",)),
        name="s5scan",
    )(ug, lamc, lamr, bb, rep, cc, ca, cbm, dsk)


def _out1_kernel(y_ref, gate_ref, x_ref, wg_ref, wo_ref, g_ref, b_ref, o_ref, v_ref, s_ref, *, alpha):
    w = y_ref.shape[1]
    d = x_ref.shape[1]
    mc = 512
    y = y_ref[0]
    for c0 in range(0, w, mc):
        ga = jnp.dot(wg_ref[c0:c0 + mc, :], y, preferred_element_type=F32)
        gb = jnp.dot(wg_ref[w + c0:w + c0 + mc, :], y, preferred_element_type=F32)
        v_ref[c0:c0 + mc, :] = (ga * _sigmoid(gb) * _silu(gate_ref[0, c0:c0 + mc, :].astype(F32))).astype(BF16)
    v = v_ref[...]
    for c0 in range(0, d, mc):
        s_ref[c0:c0 + mc, :] = alpha * x_ref[0, c0:c0 + mc, :] + jnp.dot(
            wo_ref[c0:c0 + mc, :], v, preferred_element_type=F32)
    s = s_ref[...]
    mu = jnp.mean(s, axis=0, keepdims=True)
    sc = s - mu
    var = jnp.mean(sc * sc, axis=0, keepdims=True)
    o_ref[0] = sc * lax.rsqrt(var + LN_EPS) * g_ref[...] + b_ref[...]


def _out1(y3, ug, xt, wgt, wot, g_col, b_col, *, alpha):
    l, w, r = y3.shape
    d = xt.shape[1]
    gate_blk = 1
    return pl.pallas_call(
        functools.partial(_out1_kernel, alpha=alpha),
        grid=(l,),
        in_specs=[
            pl.BlockSpec((1, w, r), lambda i: (i, 0, 0)),
            pl.BlockSpec((1, w, r), lambda i: (i, gate_blk, 0)),
            pl.BlockSpec((1, d, r), lambda i: (i, 0, 0)),
            _resident((2 * w, w), lambda i: (0, 0)),
            _resident((d, w), lambda i: (0, 0)),
            _resident((d, 1), lambda i: (0, 0)),
            _resident((d, 1), lambda i: (0, 0)),
        ],
        out_specs=pl.BlockSpec((1, d, r), lambda i: (i, 0, 0)),
        out_shape=jax.ShapeDtypeStruct((l, d, r), F32),
        scratch_shapes=[pltpu.VMEM((w, r), BF16), pltpu.VMEM((d, r), F32)],
        compiler_params=_cparams(("parallel",)),
        name="out1",
    )(y3, ug, xt, wgt, wot, g_col, b_col)


def _pad_rows(v, rows=SUBLANES):
    v = jnp.atleast_2d(v.astype(F32))
    return jnp.pad(v, ((0, rows - v.shape[0]), (0, 0)))


def _pad_lanes(v, lanes=LANES):
    return jnp.pad(v, [(0, 0)] * (v.ndim - 1) + [(0, lanes - v.shape[-1])])


def _layer0(x2, batch, seq, in_w, conv_w, conv_b, dt_bias, a_log, d_skip, norm_g, out_w, ln_g, ln_b, alpha):
    d = x2.shape[1]
    ssd_heads = dt_bias.shape[0]
    width = ssd_heads * SSD_HEAD_DIM
    conv_dim = conv_w.shape[1]
    d_state = (conv_dim - width) // (2 * SSD_GROUPS)
    att_width = (in_w.shape[1] - width - conv_dim - ssd_heads) // 4
    att_heads = att_width // ATT_HEAD_DIM

    z_end = width
    xbc_end = z_end + conv_dim
    dt_end = xbc_end + ssd_heads
    w_t = in_w.T.astype(BF16)
    w_dt = jnp.pad(w_t[xbc_end:dt_end], ((0, LANES - ssd_heads), (0, 0)))
    z_col, xbc_col = 0, z_end
    q_col = xbc_end
    k_col, v_col, g_col = q_col + att_width, q_col + 2 * att_width, q_col + 3 * att_width

    proj, dt_raw = _inproj(x2, w_t, w_dt, tm=1024, tn=1024, gap_at=xbc_end, gap=ssd_heads)

    expand = (jnp.arange(LANES)[:, None] == (jnp.arange(width)[None, :] // SSD_HEAD_DIM)).astype(BF16)
    ya = _ssd(proj, dt_raw, _pad_rows(conv_w), _pad_rows(conv_b), _pad_rows(_pad_lanes(dt_bias.astype(F32))),
              _pad_rows(_pad_lanes(-jnp.exp(a_log.astype(F32)))),
              _pad_rows(jnp.repeat(d_skip.astype(F32), SSD_HEAD_DIM)), _pad_rows(norm_g), expand,
              batch=batch, seq=seq, width=width, n_groups=SSD_GROUPS, d_state=d_state,
              z_col=z_col, xbc_col=xbc_col)

    half = ROPE_DIM // 2
    inv_freq = ROPE_THETA ** (-(jnp.arange(half, dtype=F32) * 2.0 / ROPE_DIM))
    ang = jnp.arange(seq, dtype=F32)[:, None] * inv_freq[None, :]
    ones = jnp.ones((seq, ATT_HEAD_DIM - ROPE_DIM), F32)
    cos_t = jnp.concatenate([jnp.cos(ang), jnp.cos(ang), ones], axis=1)
    sin_t = jnp.concatenate([-jnp.sin(ang), jnp.sin(ang), 0.0 * ones], axis=1)
    yb = _moba(proj, cos_t, sin_t, batch=batch, seq=seq, heads=att_heads,
               q_col=q_col, k_col=k_col, v_col=v_col, g_col=g_col, heads_per_step=4)

    return _out0(ya, yb, x2, out_w.astype(BF16), _pad_rows(ln_g), _pad_rows(ln_b), alpha=alpha, tm=512)


def _layer1(x2, batch, seq, in_w, lam_re, lam_im, log_dt, b_re, b_im, c_re, c_im, d_skip, glu_w, out_w,
            ln_g, ln_b, alpha):
    t, d = x2.shape
    chunk = S5_L
    cps = seq // chunk
    r = batch * cps
    groups, n = lam_re.shape
    m = S5_GROUP
    w = groups * m

    lre, lim = lam_re.astype(F32), lam_im.astype(F32)
    dt = jnp.exp(log_dt.astype(F32))[:, None]
    mag = jnp.exp(lre * dt)
    lbr, lbi = mag * jnp.cos(lim * dt), mag * jnp.sin(lim * dt)
    den = lre * lre + lim * lim
    fr = ((lbr - 1.0) * lre + lbi * lim) / den
    fi = (lbi * lre - (lbr - 1.0) * lim) / den
    bre, bim = b_re.astype(F32), b_im.astype(F32)
    bbr = fr[..., None] * bre - fi[..., None] * bim
    bbi = fr[..., None] * bim + fi[..., None] * bre
    lamc = jnp.broadcast_to(jnp.stack([lbr, lbi], axis=1)[..., None], (groups, 2, n, LANES))
    lamr = jnp.stack([jnp.concatenate([lbr, lbr], -1), jnp.concatenate([lbi, lbi], -1)], axis=1)
    lamr = jnp.pad(lamr, ((0, 0), (0, SUBLANES - 2), (0, 0)))
    bb = jnp.stack([bbr, bbi], axis=1)
    rep = (jnp.arange(m)[:, None] == (jnp.arange(LANES)[None, :] % m)).astype(BF16)
    cr, ci = c_re.astype(F32), c_im.astype(F32)
    cc = jnp.concatenate([cr, -ci], axis=-1)
    cbm = jnp.concatenate([-ci, -cr], axis=-1)
    dsk = jnp.broadcast_to(d_skip.astype(F32).reshape(groups, m, 1), (groups, m, LANES))

    xt = x2.reshape(r, chunk, d).transpose(1, 2, 0)
    ug = _inproj1(xt, in_w.T.astype(BF16))
    y3 = _s5(ug, lamc, lamr, bb, rep, cc, cc, cbm, dsk, chunks_per_seq=cps)
    o3 = _out1(y3, ug, xt, glu_w.T.astype(BF16), out_w.T.astype(BF16),
               ln_g.astype(F32)[:, None], ln_b.astype(F32)[:, None], alpha=alpha)
    return o3.transpose(2, 0, 1).reshape(t, d)


def kernel(x, in0_w, conv_w, conv_b, dt_bias, a_log, ssd_d, ssd_norm_g, out0_w, in1_w, s5_lam_re, s5_lam_im,
           s5_log_dt, s5_b_re, s5_b_im, s5_c_re, s5_c_im, s5_d, glu_w, out1_w, ln_g, ln_b):
    batch, seq, d = x.shape
    depth = ln_g.shape[0]
    alpha = (2 * depth) ** 0.25
    x2 = x.reshape(batch * seq, d)
    for layer in range(depth):
        i = layer // 2
        if layer % 2 == 0:
            x2 = _layer0(x2, batch, seq, in0_w[i], conv_w[i], conv_b[i], dt_bias[i], a_log[i], ssd_d[i],
                         ssd_norm_g[i], out0_w[i], ln_g[layer], ln_b[layer], alpha)
        else:
            x2 = _layer1(x2, batch, seq, in1_w[i], s5_lam_re[i], s5_lam_im[i], s5_log_dt[i], s5_b_re[i],
                         s5_b_im[i], s5_c_re[i], s5_c_im[i], s5_d[i], glu_w[i], out1_w[i],
                         ln_g[layer], ln_b[layer], alpha)
    return x2.reshape(batch, seq, d).astype(x.dtype)
```

```python
import functools
import math

import jax
import jax.numpy as jnp
from jax import lax
from jax.experimental import pallas as pl
from jax.experimental.pallas import tpu as pltpu

F32 = jnp.float32
BF16 = jnp.bfloat16

SSD_HEAD_DIM = 64
SSD_GROUPS = 4
SSD_STATE = 128
SSD_CONV = 4
SSD_CHUNK = 256
ATT_HEAD_DIM = 128
MOBA_BLOCK = 256
MOBA_TOPK = 3
ROPE_THETA = 500000.0
ROPE_DIM = ATT_HEAD_DIM // 4
S5_GROUP = 16
S5_STATE = 64
LN_EPS = 1e-5
RMS_EPS = 1e-5
NEG_INF = -1e30

LANES = 128
SUBLANES = 8
VMEM_LIMIT = 56 * 1024 * 1024

S5_L = 64


def _cparams(sem):
    return pltpu.CompilerParams(dimension_semantics=sem, vmem_limit_bytes=VMEM_LIMIT)


def _resident(shape, index_map):
    return pl.BlockSpec(shape, index_map, pipeline_mode=pl.Buffered(1))


def _sigmoid(v):
    return 0.5 + 0.5 * jnp.tanh(0.5 * v)


def _silu(v):
    h = 0.5 * v
    return h + h * jnp.tanh(h)


def _nt_dot(a, b):
    return lax.dot_general(a, b, (((1,), (1,)), ((), ())), preferred_element_type=F32)


def _tn_dot(a, b):
    return lax.dot_general(a, b, (((0,), (0,)), ((), ())), preferred_element_type=F32)


def _dot_hi(a, b):
    a_hi = a.astype(BF16)
    b_hi = b.astype(BF16)
    a_lo = (a - a_hi.astype(F32)).astype(BF16)
    b_lo = (b - b_hi.astype(F32)).astype(BF16)
    return (jnp.dot(a_hi, b_hi, preferred_element_type=F32) + jnp.dot(a_hi, b_lo, preferred_element_type=F32)
            + jnp.dot(a_lo, b_hi, preferred_element_type=F32))


def _dot_split(a, b, passes, split_lhs=True):
    rem = a if split_lhs else b
    acc = None
    for _ in range(passes):
        piece = rem.astype(BF16)
        part = jnp.dot(piece, b, preferred_element_type=F32) if split_lhs else jnp.dot(
            a, piece, preferred_element_type=F32)
        acc = part if acc is None else acc + part
        rem = rem - piece.astype(F32)
    return acc


def _inproj_kernel(x_ref, w_ref, wdt_ref, o_ref, dt_ref, xb_ref):
    @pl.when(pl.program_id(1) == 0)
    def _():
        xb = x_ref[...].astype(BF16)
        xb_ref[...] = xb
        dt_ref[...] = _nt_dot(xb, wdt_ref[...])

    o_ref[...] = _nt_dot(xb_ref[...], w_ref[...]).astype(o_ref.dtype)


def _inproj(x2, w_t, w_dt_t, tm, tn, gap_at, gap):
    t, d = x2.shape
    n = w_t.shape[0] - gap
    assert gap_at % tn == 0 and n % tn == 0
    return pl.pallas_call(
        _inproj_kernel,
        grid=(t // tm, n // tn),
        in_specs=[
            pl.BlockSpec((tm, d), lambda i, j: (i, 0)),
            pl.BlockSpec((pl.Element(tn), pl.Element(d)),
                         lambda i, j: (pl.multiple_of(j * tn + jnp.where(j * tn >= gap_at, gap, 0),
                                                      math.gcd(tn, gap)), 0)),
            _resident((LANES, d), lambda i, j: (0, 0)),
        ],
        out_specs=[
            pl.BlockSpec((tm, tn), lambda i, j: (i, j)),
            pl.BlockSpec((tm, LANES), lambda i, j: (i, 0)),
        ],
        out_shape=[jax.ShapeDtypeStruct((t, n), BF16), jax.ShapeDtypeStruct((t, LANES), F32)],
        scratch_shapes=[pltpu.VMEM((tm, d), BF16)],
        compiler_params=_cparams(("parallel", "arbitrary")),
        name="inproj0",
    )(x2, w_t, w_dt_t)


def _ssd_kernel(z_ref, xs_ref, bm_ref, cm_ref, dt_ref, cw_ref, cb_ref, dtb_ref, aneg_ref, dsk_ref, ng_ref,
                exp_ref, o_ref, xpad_ref, state_ref, act_ref, y_ref, *, n_groups, head_dim):
    c = pl.program_id(1)
    chunk, width = z_ref.shape
    d_state = bm_ref.shape[1] // n_groups
    heads_per_group = width // head_dim // n_groups
    gw = heads_per_group * head_dim

    @pl.when(c == 0)
    def _():
        xpad_ref[0:SUBLANES, :] = jnp.zeros((SUBLANES, xpad_ref.shape[1]), F32)
        state_ref[...] = jnp.zeros_like(state_ref)

    strip = 512
    col0 = 0
    for src in (xs_ref, bm_ref, cm_ref):
        for s0 in range(0, src.shape[1], strip):
            cols = slice(col0 + s0, col0 + s0 + strip)
            xs_ = src[:, s0:s0 + strip].astype(F32)
            xpad_ref[SUBLANES:, cols] = xs_
            acc = cb_ref[0:1, cols] + cw_ref[SSD_CONV - 1:SSD_CONV, cols] * xs_
            for k in range(1, SSD_CONV):
                acc = acc + cw_ref[SSD_CONV - 1 - k:SSD_CONV - k, cols] * xpad_ref[SUBLANES - k:SUBLANES - k + chunk, cols]
            act_ref[:, cols] = _silu(acc)
            xpad_ref[0:SUBLANES, cols] = xs_[chunk - SUBLANES:, :]
        col0 += src.shape[1]

    dt_in = dt_ref[...] + dtb_ref[0:1, :]
    dt = jnp.maximum(dt_in, 0.0) + jnp.log(1.0 + jnp.exp(-jnp.abs(dt_in)))
    a = dt * aneg_ref[0:1, :]
    row = lax.broadcasted_iota(jnp.int32, (chunk, chunk), 0)
    col = lax.broadcasted_iota(jnp.int32, (chunk, chunk), 1)
    causal = row >= col
    cs = _dot_split(jnp.where(causal, 1.0, 0.0).astype(BF16), a, 3, split_lhs=False)
    cs2 = cs * math.log2(math.e)
    cs2_t = cs2.T
    cs_last = cs[chunk - 1:chunk, :]
    expand = exp_ref[...]
    dt_x = _dot_split(dt, expand, 2)
    ecs_x = _dot_split(jnp.exp(cs), expand, 2)
    dec_x = _dot_split(jnp.exp(cs_last - cs), expand, 2)
    sdec_x = _dot_split(jnp.broadcast_to(jnp.exp(cs_last), (SUBLANES, LANES)), expand, 2)[0:1, :]

    lane = lax.broadcasted_iota(jnp.int32, (chunk, LANES), 1)
    first_half = lane < head_dim
    for g in range(n_groups):
        gs = slice(g * gw, (g + 1) * gw)
        b_g = act_ref[:, width + g * d_state:width + (g + 1) * d_state].astype(BF16)
        c_g = act_ref[:, width + (n_groups + g) * d_state:width + (n_groups + g + 1) * d_state].astype(BF16)
        xs_g = act_ref[:, gs]
        xdt = xs_g * dt_x[:, gs]
        xdt_b = xdt.astype(BF16)
        cb = jnp.where(causal, _nt_dot(c_g, b_g), 0.0)
        st = state_ref[:, gs]
        y_g = jnp.dot(c_g, st.astype(BF16), preferred_element_type=F32) * ecs_x[:, gs] + dsk_ref[0:1, gs] * xs_g
        state_ref[:, gs] = st * sdec_x[:, gs] + _tn_dot(b_g, (xdt * dec_x[:, gs]).astype(BF16))
        for pr in range(heads_per_group // 2):
            xp = xdt_b[:, pr * LANES:(pr + 1) * LANES]
            halves = []
            for half in range(2):
                h = g * heads_per_group + 2 * pr + half
                seg = cs2[:, h:h + 1] - cs2_t[h:h + 1, :]
                m = (cb * jnp.exp2(jnp.minimum(seg, 0.0))).astype(BF16)
                halves.append(jnp.dot(m, xp, preferred_element_type=F32))
            y_pair = jnp.where(first_half, halves[0], halves[1])
            y_ref[:, g * gw + pr * LANES:g * gw + (pr + 1) * LANES] = y_pair + y_g[:, pr * LANES:(pr + 1) * LANES]

    yz = y_ref[...] * _silu(z_ref[...].astype(F32))
    ms = jnp.mean(yz * yz, axis=1, keepdims=True)
    o_ref[...] = (yz * lax.rsqrt(ms + RMS_EPS) * ng_ref[0:1, :]).astype(o_ref.dtype)


def _ssd(proj, dt_raw, cw, cb, dtb, aneg, dsk, ng, expand, *, batch, seq, width, n_groups, d_state,
         z_col, xbc_col):
    conv_dim = width + 2 * n_groups * d_state
    bc_w = n_groups * d_state
    nc = seq // SSD_CHUNK
    t = batch * seq
    kern = functools.partial(_ssd_kernel, n_groups=n_groups, head_dim=SSD_HEAD_DIM)
    b_col, c_col = xbc_col + width, xbc_col + width + bc_w
    assert z_col % width == 0 and xbc_col % width == 0 and b_col % bc_w == 0 and c_col % bc_w == 0
    const = lambda b, c: (0, 0)
    return pl.pallas_call(
        kern,
        grid=(batch, nc),
        in_specs=[
            pl.BlockSpec((SSD_CHUNK, width), lambda b, c: (b * nc + c, z_col // width)),
            pl.BlockSpec((SSD_CHUNK, width), lambda b, c: (b * nc + c, xbc_col // width)),
            pl.BlockSpec((SSD_CHUNK, bc_w), lambda b, c: (b * nc + c, b_col // bc_w)),
            pl.BlockSpec((SSD_CHUNK, bc_w), lambda b, c: (b * nc + c, c_col // bc_w)),
            pl.BlockSpec((SSD_CHUNK, LANES), lambda b, c: (b * nc + c, 0)),
            _resident((SUBLANES, conv_dim), const),
            _resident((SUBLANES, conv_dim), const),
            _resident((SUBLANES, LANES), const),
            _resident((SUBLANES, LANES), const),
            _resident((SUBLANES, width), const),
            _resident((SUBLANES, width), const),
            _resident((LANES, width), const),
        ],
        out_specs=pl.BlockSpec((SSD_CHUNK, width), lambda b, c: (b * nc + c, 0)),
        out_shape=jax.ShapeDtypeStruct((t, width), BF16),
        scratch_shapes=[
            pltpu.VMEM((SUBLANES + SSD_CHUNK, conv_dim), F32),
            pltpu.VMEM((d_state, width), F32),
            pltpu.VMEM((SSD_CHUNK, conv_dim), F32),
            pltpu.VMEM((SSD_CHUNK, width), F32),
        ],
        compiler_params=_cparams(("parallel", "arbitrary")),
        name="ssd0",
    )(proj, proj, proj, proj, dt_raw, cw, cb, dtb, aneg, dsk, ng, expand)


def _rope(v, cos, sin):
    half = ROPE_DIM // 2
    lane = lax.broadcasted_iota(jnp.int32, v.shape, 1)
    partner = jnp.where(lane < half, pltpu.roll(v, LANES - half, 1), pltpu.roll(v, half, 1))
    return v * cos + partner * sin


def _moba_kernel(q_ref, k_ref, v_ref, g_ref, cos_ref, sin_ref, o_ref, kr_ref, vt_ref, km_ref, ch_ref,
                 sa_ref, sb_ref, *, n_blocks, heads_per_step):
    qb = pl.program_id(2)
    blk = MOBA_BLOCK
    hd = ATT_HEAD_DIM
    scale = hd ** -0.5 * math.log2(math.e)
    heads = range(heads_per_step)

    n_rows = km_ref.shape[1]

    @pl.when(qb == 0)
    def _():
        km_ref[...] = jnp.zeros_like(km_ref)
        for n in range(n_blocks):
            rows = slice(n * blk, (n + 1) * blk)
            for j in heads:
                cols = slice(j * hd, (j + 1) * hd)
                kr = _rope(k_ref[rows, cols].astype(F32), cos_ref[rows, :], sin_ref[rows, :])
                kr_ref[j, rows, :] = kr.astype(BF16)
                if n + 1 < n_rows:
                    km_ref[j, n + 1:n + 2, :] = jnp.mean(kr, axis=0, keepdims=True)
                vt_ref[j, 0:hd, rows] = v_ref[rows, cols].astype(F32).T.astype(BF16)
                vt_ref[j, hd:, rows] = jnp.ones((vt_ref.shape[1] - hd, blk), BF16)

    q0 = pl.multiple_of(qb * blk, blk)
    cos_q = cos_ref[pl.ds(q0, blk), :]
    sin_q = sin_ref[pl.ds(q0, blk), :]
    row_id = lax.broadcasted_iota(jnp.int32, (n_rows, blk), 0)
    past = (row_id >= 1) & (row_id <= qb)
    key_i = lax.broadcasted_iota(jnp.int32, (blk, blk), 0)
    qry_i = lax.broadcasted_iota(jnp.int32, (blk, blk), 1)

    qs_t = []
    for j in heads:
        qr_t = _rope(q_ref[:, j * hd:(j + 1) * hd].astype(F32), cos_q, sin_q).T
        qs_t.append((qr_t * scale).astype(BF16))

        s_own = jnp.dot(kr_ref[j, pl.ds(q0, blk), :], qs_t[j], preferred_element_type=F32)
        sa_ref[j] = jnp.where(key_i <= qry_i, s_own, NEG_INF)

        gate = jnp.where(past, _dot_hi(km_ref[j], qr_t), NEG_INF)
        rank = jnp.zeros((n_rows, blk), jnp.int32)
        for n2 in range(1, n_blocks):
            other = gate[n2:n2 + 1, :]
            rank = rank + jnp.where(other > gate, 1,
                                    jnp.where(other == gate, jnp.where(row_id > n2, 1, 0), 0))
        ch_ref[j] = jnp.where(row_id == 0, 1.0, jnp.where(past, jnp.where(rank < MOBA_TOPK, 1.0, 0.0), 0.0))

    def stage(k, ref):
        n0 = pl.multiple_of(jnp.clip(k - 1, 0, n_blocks - 1) * blk, blk)
        for j in heads:
            ref[j] = jnp.dot(kr_ref[j, pl.ds(n0, blk), :], qs_t[j], preferred_element_type=F32)

    def consume(k, ref, carry):
        n0 = pl.multiple_of(jnp.where(k == 0, qb, k - 1) * blk, blk)
        out = []
        for j in heads:
            m, acc = carry[j]
            taken = ch_ref[j, pl.ds(k, 1), :] > 0.5
            m_new = jnp.maximum(m, jnp.where(taken, jnp.max(ref[j], axis=0, keepdims=True), NEG_INF))
            alpha = jnp.exp2(m - m_new)
            p = jnp.exp2(ref[j] - jnp.where(taken, m_new, -NEG_INF)).astype(BF16)
            out.append((m_new, alpha * acc + jnp.dot(vt_ref[j, :, pl.ds(n0, blk)], p,
                                                     preferred_element_type=F32)))
        return tuple(out)

    def two_blocks(i, carry):
        stage(2 * i + 1, sb_ref)
        carry = consume(2 * i, sa_ref, carry)
        stage(2 * i + 2, sa_ref)
        return consume(2 * i + 1, sb_ref, carry)

    init = tuple((jnp.full((1, blk), NEG_INF, F32), jnp.zeros((vt_ref.shape[1], blk), F32)) for _ in heads)
    final = lax.fori_loop(0, qb // 2 + 1, two_blocks, init)
    for j in heads:
        _, acc = final[j]
        cols = slice(j * hd, (j + 1) * hd)
        out_t = acc[0:hd, :] / acc[hd:hd + 1, :]
        o_ref[:, cols] = (out_t.T * _silu(g_ref[:, cols].astype(F32))).astype(o_ref.dtype)


def _moba(proj, cos_t, sin_t, *, batch, seq, heads, q_col, k_col, v_col, g_col, heads_per_step):
    nb = seq // MOBA_BLOCK
    hd = ATT_HEAD_DIM
    hw = hd * heads_per_step
    t = batch * seq
    assert heads % heads_per_step == 0 and all(c % hw == 0 for c in (q_col, k_col, v_col, g_col))
    qc, kc, vc, gc = (c // hw for c in (q_col, k_col, v_col, g_col))
    kern = functools.partial(_moba_kernel, n_blocks=nb, heads_per_step=heads_per_step)
    return pl.pallas_call(
        kern,
        grid=(batch, heads // heads_per_step, nb),
        in_specs=[
            pl.BlockSpec((MOBA_BLOCK, hw), lambda b, h, i: (b * nb + i, qc + h)),
            pl.BlockSpec((seq, hw), lambda b, h, i: (b, kc + h)),
            pl.BlockSpec((seq, hw), lambda b, h, i: (b, vc + h)),
            pl.BlockSpec((MOBA_BLOCK, hw), lambda b, h, i: (b * nb + i, gc + h)),
            _resident((seq, hd), lambda b, h, i: (0, 0)),
            _resident((seq, hd), lambda b, h, i: (0, 0)),
        ],
        out_specs=pl.BlockSpec((MOBA_BLOCK, hw), lambda b, h, i: (b * nb + i, h)),
        out_shape=jax.ShapeDtypeStruct((t, heads * hd), BF16),
        scratch_shapes=[
            pltpu.VMEM((heads_per_step, seq, hd), BF16),
            pltpu.VMEM((heads_per_step, hd + 16, seq), BF16),
            pltpu.VMEM((heads_per_step, nb, hd), F32),
            pltpu.VMEM((heads_per_step, nb, MOBA_BLOCK), F32),
            pltpu.VMEM((heads_per_step, MOBA_BLOCK, MOBA_BLOCK), F32),
            pltpu.VMEM((heads_per_step, MOBA_BLOCK, MOBA_BLOCK), F32),
        ],
        compiler_params=_cparams(("parallel", "parallel", "arbitrary")),
        name="moba0",
    )(proj, proj, proj, proj, cos_t, sin_t)


def _out0_kernel(ya_ref, yb_ref, x_ref, w_ref, g_ref, b_ref, o_ref, *, alpha):
    ka = ya_ref.shape[1]
    h = jnp.dot(ya_ref[...], w_ref[0:ka, :], preferred_element_type=F32)
    h = h + jnp.dot(yb_ref[...], w_ref[ka:, :], preferred_element_type=F32)
    v = alpha * x_ref[...] + h
    mu = jnp.mean(v, axis=1, keepdims=True)
    vc = v - mu
    var = jnp.mean(vc * vc, axis=1, keepdims=True)
    o_ref[...] = vc * lax.rsqrt(var + LN_EPS) * g_ref[0:1, :] + b_ref[0:1, :]


def _out0(ya, yb, x2, w, g, b, *, alpha, tm):
    t, d = x2.shape
    ka, kb = ya.shape[1], yb.shape[1]
    return pl.pallas_call(
        functools.partial(_out0_kernel, alpha=alpha),
        grid=(t // tm,),
        in_specs=[
            pl.BlockSpec((tm, ka), lambda i: (i, 0)),
            pl.BlockSpec((tm, kb), lambda i: (i, 0)),
            pl.BlockSpec((tm, d), lambda i: (i, 0)),
            _resident((ka + kb, d), lambda i: (0, 0)),
            _resident((SUBLANES, d), lambda i: (0, 0)),
            _resident((SUBLANES, d), lambda i: (0, 0)),
        ],
        out_specs=pl.BlockSpec((tm, d), lambda i: (i, 0)),
        out_shape=jax.ShapeDtypeStruct((t, d), F32),
        compiler_params=_cparams(("parallel",)),
        name="out0",
    )(ya, yb, x2, w, g, b)


def _inproj1_kernel(x_ref, w_ref, o_ref):
    for s in range(x_ref.shape[0]):
        o_ref[s] = jnp.dot(w_ref[...], x_ref[s].astype(BF16), preferred_element_type=F32).astype(o_ref.dtype)


def _inproj1(xt, wt, slabs_per_step=2):
    l, d, r = xt.shape
    n = wt.shape[0]
    sps = slabs_per_step
    return pl.pallas_call(
        _inproj1_kernel,
        grid=(l // sps,),
        in_specs=[pl.BlockSpec((sps, d, r), lambda i: (i, 0, 0)), _resident((n, d), lambda i: (0, 0))],
        out_specs=pl.BlockSpec((sps, n, r), lambda i: (i, 0, 0)),
        out_shape=jax.ShapeDtypeStruct((l, n, r), BF16),
        compiler_params=_cparams(("parallel",)),
        name="inproj1",
    )(xt, wt)


def _cpow(lr, li, d, nbits):
    pr = jnp.ones(d.shape, F32)
    pi = jnp.zeros(d.shape, F32)
    br, bi = lr, li
    for bit in range(nbits):
        on = ((d >> bit) & 1) == 1
        nr = pr * br - pi * bi
        ni = pr * bi + pi * br
        pr = jnp.where(on, nr, pr)
        pi = jnp.where(on, ni, pi)
        if bit + 1 < nbits:
            br, bi = br * br - bi * bi, 2.0 * br * bi
    return pr, pi


def _gelu_tanh(v):
    c = math.sqrt(2.0 / math.pi)
    h = 0.5 * v
    return h + h * jnp.tanh(v * (c + (c * 0.044715) * (v * v)))


def _s5_group(j, u_ref, lamc_ref, lamr_ref, bb_ref, rep_ref, cc_ref, ca_ref, cbm_ref, dsk_ref, o_ref, tz_ref,
              *, chunk, chunks_per_seq):
    m = S5_GROUP
    n = S5_STATE
    lm = chunk * m
    r = u_ref.shape[2]
    u = u_ref[:, j * m:(j + 1) * m, :].reshape(lm, r)

    pos_per_blk = LANES // m
    n_lane_blk = lm // LANES
    lam_r, lam_i = lamc_ref[j, 0], lamc_ref[j, 1]
    lane = lax.broadcasted_iota(jnp.int32, (n, LANES), 1)
    pr, pi = _cpow(lam_r, lam_i, (pos_per_blk - 1) - lane // m, (pos_per_blk - 1).bit_length())
    lbr, lbi = lam_r, lam_i
    for _ in range(pos_per_blk.bit_length() - 1):
        lbr, lbi = lbr * lbr - lbi * lbi, 2.0 * lbr * lbi
    bbr = _dot_split(bb_ref[j, 0], rep_ref[...], 3)
    bbi = _dot_split(bb_ref[j, 1], rep_ref[...], 3)
    bl_r, bl_i = [None] * n_lane_blk, [None] * n_lane_blk
    for c in range(n_lane_blk - 1, -1, -1):
        bl_r[c] = pr * bbr - pi * bbi
        bl_i[c] = pr * bbi + pi * bbr
        pr, pi = pr * lbr - pi * lbi, pr * lbi + pi * lbr
    bl = jnp.concatenate([jnp.concatenate(bl_r, axis=1), jnp.concatenate(bl_i, axis=1)], axis=0)

    krev = _dot_hi(cc_ref[j], bl)
    col = lax.broadcasted_iota(jnp.int32, (m, lm), 1)
    row = lax.broadcasted_iota(jnp.int32, (m, lm), 0)
    d_lanes = jnp.concatenate([dsk_ref[j]] * n_lane_blk, axis=1)
    krev = krev + jnp.where(col == (chunk - 1) * m + row, d_lanes, 0.0)

    col_s = col // m
    for t in range(chunk):
        rolled = pltpu.roll(krev, ((t + 1) * m) % lm, 1)
        tz_ref[j, t * m:(t + 1) * m, 0:lm] = jnp.where(col_s <= t, rolled, 0.0).astype(BF16)

    e = jnp.dot(bl.astype(BF16), u, preferred_element_type=F32)
    lane_r = lax.broadcasted_iota(jnp.int32, (n, r), 1) % chunks_per_seq
    hr = jnp.where(lane_r >= 1, pltpu.roll(e[:n], 1, 1), 0.0)
    hi = jnp.where(lane_r >= 1, pltpu.roll(e[n:], 1, 1), 0.0)
    mr, mi = lbr, lbi
    for _ in range((chunk // pos_per_blk).bit_length() - 1):
        mr, mi = mr * mr - mi * mi, 2.0 * mr * mi
    mr = jnp.concatenate([mr] * (r // LANES), axis=1) if r > LANES else mr[:, :r]
    mi = jnp.concatenate([mi] * (r // LANES), axis=1) if r > LANES else mi[:, :r]
    step = 1
    while step < chunks_per_seq:
        sr = jnp.where(lane_r >= step, pltpu.roll(hr, step, 1), 0.0)
        si = jnp.where(lane_r >= step, pltpu.roll(hi, step, 1), 0.0)
        hr, hi = hr + mr * sr - mi * si, hi + mr * si + mi * sr
        mr, mi = mr * mr - mi * mi, 2.0 * mr * mi
        step *= 2
    h_in = jnp.concatenate([hr, hi], axis=0).astype(BF16)

    lam_rr = jnp.broadcast_to(lamr_ref[j, 0:1, :], (m, 2 * n))
    lam_ri = jnp.broadcast_to(lamr_ref[j, 1:2, :], (m, 2 * n))
    ca, cbm = ca_ref[j], cbm_ref[j]
    qr, qi = lam_rr, lam_ri
    for t in range(chunk):
        tz_ref[j, t * m:(t + 1) * m, lm:] = (ca * qr + cbm * qi).astype(BF16)
        if t + 1 < chunk:
            qr, qi = qr * lam_rr - qi * lam_ri, qr * lam_ri + qi * lam_rr

    y = jnp.dot(tz_ref[j], jnp.concatenate([u, h_in], axis=0), preferred_element_type=F32)
    o_ref[:, j * m:(j + 1) * m, :] = _gelu_tanh(y).astype(o_ref.dtype).reshape(chunk, m, r)


def _s5_kernel(*refs, chunk, chunks_per_seq, groups_per_step):
    for j in range(groups_per_step):
        _s5_group(j, *refs, chunk=chunk, chunks_per_seq=chunks_per_seq)


def _s5(ug, lamc, lamr, bb, rep, cc, ca, cbm, dsk, *, chunks_per_seq):
    chunk, _, r = ug.shape
    g = lamc.shape[0]
    m, n = S5_GROUP, S5_STATE
    lm = chunk * m
    gps = 2
    kern = functools.partial(_s5_kernel, chunk=chunk, chunks_per_seq=chunks_per_seq, groups_per_step=gps)
    p4 = lambda i: (i, 0, 0, 0)
    p3 = lambda i: (i, 0, 0)
    return pl.pallas_call(
        kern,
        grid=(g // gps,),
        in_specs=[
            pl.BlockSpec((chunk, gps * m, r), lambda i: (0, i, 0)),
            pl.BlockSpec((gps, 2, n, LANES), p4),
            pl.BlockSpec((gps, SUBLANES, 2 * n), p3),
            pl.BlockSpec((gps, 2, n, m), p4),
            _resident((m, LANES), lambda i: (0, 0)),
            pl.BlockSpec((gps, m, 2 * n), p3),
            pl.BlockSpec((gps, m, 2 * n), p3),
            pl.BlockSpec((gps, m, 2 * n), p3),
            pl.BlockSpec((gps, m, LANES), p3),
        ],
        out_specs=pl.BlockSpec((chunk, gps * m, r), lambda i: (0, i, 0)),
        out_shape=jax.ShapeDtypeStruct((chunk, g * m, r), BF16),
        scratch_shapes=[pltpu.VMEM((gps, lm, lm + 2 * n), BF16)],
        compiler_params=_cparams(("parallel",)),
        name="s5scan",
    )(ug, lamc, lamr, bb, rep, cc, ca, cbm, dsk)


def _out1_kernel(y_ref, gate_ref, x_ref, wg_ref, wo_ref, g_ref, b_ref, o_ref, v_ref, s_ref, *, alpha):
    w = y_ref.shape[1]
    d = x_ref.shape[1]
    mc = 512
    y = y_ref[0]
    for c0 in range(0, w, mc):
        ga = jnp.dot(wg_ref[c0:c0 + mc, :], y, preferred_element_type=F32)
        gb = jnp.dot(wg_ref[w + c0:w + c0 + mc, :], y, preferred_element_type=F32)
        v_ref[c0:c0 + mc, :] = (ga * _sigmoid(gb) * _silu(gate_ref[0, c0:c0 + mc, :].astype(F32))).astype(BF16)
    v = v_ref[...]
    for c0 in range(0, d, mc):
        s_ref[c0:c0 + mc, :] = alpha * x_ref[0, c0:c0 + mc, :] + jnp.dot(
            wo_ref[c0:c0 + mc, :], v, preferred_element_type=F32)
    s = s_ref[...]
    mu = jnp.mean(s, axis=0, keepdims=True)
    sc = s - mu
    var = jnp.mean(sc * sc, axis=0, keepdims=True)
    o_ref[0] = sc * lax.rsqrt(var + LN_EPS) * g_ref[...] + b_ref[...]


def _out1(y3, ug, xt, wgt, wot, g_col, b_col, *, alpha):
    l, w, r = y3.shape
    d = xt.shape[1]
    gate_blk = 1
    return pl.pallas_call(
        functools.partial(_out1_kernel, alpha=alpha),
        grid=(l,),
        in_specs=[
            pl.BlockSpec((1, w, r), lambda i: (i, 0, 0)),
            pl.BlockSpec((1, w, r), lambda i: (i, gate_blk, 0)),
            pl.BlockSpec((1, d, r), lambda i: (i, 0, 0)),
            _resident((2 * w, w), lambda i: (0, 0)),
            _resident((d, w), lambda i: (0, 0)),
            _resident((d, 1), lambda i: (0, 0)),
            _resident((d, 1), lambda i: (0, 0)),
        ],
        out_specs=pl.BlockSpec((1, d, r), lambda i: (i, 0, 0)),
        out_shape=jax.ShapeDtypeStruct((l, d, r), F32),
        scratch_shapes=[pltpu.VMEM((w, r), BF16), pltpu.VMEM((d, r), F32)],
        compiler_params=_cparams(("parallel",)),
        name="out1",
    )(y3, ug, xt, wgt, wot, g_col, b_col)


def _pad_rows(v, rows=SUBLANES):
    v = jnp.atleast_2d(v.astype(F32))
    return jnp.pad(v, ((0, rows - v.shape[0]), (0, 0)))


def _pad_lanes(v, lanes=LANES):
    return jnp.pad(v, [(0, 0)] * (v.ndim - 1) + [(0, lanes - v.shape[-1])])


def _layer0(x2, batch, seq, in_w, conv_w, conv_b, dt_bias, a_log, d_skip, norm_g, out_w, ln_g, ln_b, alpha):
    d = x2.shape[1]
    ssd_heads = dt_bias.shape[0]
    width = ssd_heads * SSD_HEAD_DIM
    conv_dim = conv_w.shape[1]
    d_state = (conv_dim - width) // (2 * SSD_GROUPS)
    att_width = (in_w.shape[1] - width - conv_dim - ssd_heads) // 4
    att_heads = att_width // ATT_HEAD_DIM

    z_end = width
    xbc_end = z_end + conv_dim
    dt_end = xbc_end + ssd_heads
    w_t = in_w.T.astype(BF16)
    w_dt = jnp.pad(w_t[xbc_end:dt_end], ((0, LANES - ssd_heads), (0, 0)))
    z_col, xbc_col = 0, z_end
    q_col = xbc_end
    k_col, v_col, g_col = q_col + att_width, q_col + 2 * att_width, q_col + 3 * att_width

    proj, dt_raw = _inproj(x2, w_t, w_dt, tm=1024, tn=1024, gap_at=xbc_end, gap=ssd_heads)

    expand = (jnp.arange(LANES)[:, None] == (jnp.arange(width)[None, :] // SSD_HEAD_DIM)).astype(BF16)
    ya = _ssd(proj, dt_raw, _pad_rows(conv_w), _pad_rows(conv_b), _pad_rows(_pad_lanes(dt_bias.astype(F32))),
              _pad_rows(_pad_lanes(-jnp.exp(a_log.astype(F32)))),
              _pad_rows(jnp.repeat(d_skip.astype(F32), SSD_HEAD_DIM)), _pad_rows(norm_g), expand,
              batch=batch, seq=seq, width=width, n_groups=SSD_GROUPS, d_state=d_state,
              z_col=z_col, xbc_col=xbc_col)

    half = ROPE_DIM // 2
    inv_freq = ROPE_THETA ** (-(jnp.arange(half, dtype=F32) * 2.0 / ROPE_DIM))
    ang = jnp.arange(seq, dtype=F32)[:, None] * inv_freq[None, :]
    ones = jnp.ones((seq, ATT_HEAD_DIM - ROPE_DIM), F32)
    cos_t = jnp.concatenate([jnp.cos(ang), jnp.cos(ang), ones], axis=1)
    sin_t = jnp.concatenate([-jnp.sin(ang), jnp.sin(ang), 0.0 * ones], axis=1)
    yb = _moba(proj, cos_t, sin_t, batch=batch, seq=seq, heads=att_heads,
               q_col=q_col, k_col=k_col, v_col=v_col, g_col=g_col, heads_per_step=4)

    return _out0(ya, yb, x2, out_w.astype(BF16), _pad_rows(ln_g), _pad_rows(ln_b), alpha=alpha, tm=512)


def _layer1(x2, batch, seq, in_w, lam_re, lam_im, log_dt, b_re, b_im, c_re, c_im, d_skip, glu_w, out_w,
            ln_g, ln_b, alpha):
    t, d = x2.shape
    chunk = S5_L
    cps = seq // chunk
    r = batch * cps
    groups, n = lam_re.shape
    m = S5_GROUP
    w = groups * m

    lre, lim = lam_re.astype(F32), lam_im.astype(F32)
    dt = jnp.exp(log_dt.astype(F32))[:, None]
    mag = jnp.exp(lre * dt)
    lbr, lbi = mag * jnp.cos(lim * dt), mag * jnp.sin(lim * dt)
    den = lre * lre + lim * lim
    fr = ((lbr - 1.0) * lre + lbi * lim) / den
    fi = (lbi * lre - (lbr - 1.0) * lim) / den
    bre, bim = b_re.astype(F32), b_im.astype(F32)
    bbr = fr[..., None] * bre - fi[..., None] * bim
    bbi = fr[..., None] * bim + fi[..., None] * bre
    lamc = jnp.broadcast_to(jnp.stack([lbr, lbi], axis=1)[..., None], (groups, 2, n, LANES))
    lamr = jnp.stack([jnp.concatenate([lbr, lbr], -1), jnp.concatenate([lbi, lbi], -1)], axis=1)
    lamr = jnp.pad(lamr, ((0, 0), (0, SUBLANES - 2), (0, 0)))
    bb = jnp.stack([bbr, bbi], axis=1)
    rep = (jnp.arange(m)[:, None] == (jnp.arange(LANES)[None, :] % m)).astype(BF16)
    cr, ci = c_re.astype(F32), c_im.astype(F32)
    cc = jnp.concatenate([cr, -ci], axis=-1)
    cbm = jnp.concatenate([-ci, -cr], axis=-1)
    dsk = jnp.broadcast_to(d_skip.astype(F32).reshape(groups, m, 1), (groups, m, LANES))

    xt = x2.reshape(r, chunk, d).transpose(1, 2, 0)
    ug = _inproj1(xt, in_w.T.astype(BF16))
    y3 = _s5(ug, lamc, lamr, bb, rep, cc, cc, cbm, dsk, chunks_per_seq=cps)
    o3 = _out1(y3, ug, xt, glu_w.T.astype(BF16), out_w.T.astype(BF16),
               ln_g.astype(F32)[:, None], ln_b.astype(F32)[:, None], alpha=alpha)
    return o3.transpose(2, 0, 1).reshape(t, d)


def kernel(x, in0_w, conv_w, conv_b, dt_bias, a_log, ssd_d, ssd_norm_g, out0_w, in1_w, s5_lam_re, s5_lam_im,
           s5_log_dt, s5_b_re, s5_b_im, s5_c_re, s5_c_im, s5_d, glu_w, out1_w, ln_g, ln_b):
    batch, seq, d = x.shape
    depth = ln_g.shape[0]
    alpha = (2 * depth) ** 0.25
    x2 = x.reshape(batch * seq, d)
    for layer in range(depth):
        i = layer // 2
        if layer % 2 == 0:
            x2 = _layer0(x2, batch, seq, in0_w[i], conv_w[i], conv_b[i], dt_bias[i], a_log[i], ssd_d[i],
                         ssd_norm_g[i], out0_w[i], ln_g[layer], ln_b[layer], alpha)
        else:
            x2 = _layer1(x2, batch, seq, in1_w[i], s5_lam_re[i], s5_lam_im[i], s5_log_dt[i], s5_b_re[i],
                         s5_b_im[i], s5_c_re[i], s5_c_im[i], s5_d[i], glu_w[i], out1_w[i],
                         ln_g[layer], ln_b[layer], alpha)
    return x2.reshape(batch, seq, d).astype(x.dtype)
```
